```python
import jax, jax.numpy as jnp
from jax import lax
import numpy as np

D_MODEL = 1024
BATCH = 4
SEQ = 8192
DEPTH = 2

GRID_W = 64
CTX_LEN = 256
EPS = 1e-6
NEG_INF = -1e30
ROPE_BASE = 10000.0

SSD_D_INNER = 1024
SSD_HEAD_DIM = 64
SSD_HEADS = SSD_D_INNER // SSD_HEAD_DIM
SSD_GROUPS = 2
SSD_HEADS_PER_GROUP = SSD_HEADS // SSD_GROUPS
SSD_STATE = 64
SSD_CONV = 5
SSD_CHUNK = 128
SSD_CONV_CH = SSD_D_INNER + 2 * SSD_GROUPS * SSD_STATE

SWA_HEADS = 8
SWA_KV_HEADS = 2
SWA_HEAD_DIM = 64
SWA_WINDOW = 128
SWA_BLOCK = 128
SWA_SCALE = SWA_HEAD_DIM ** -0.5

MLA_HEADS = 8
MLA_Q_RANK = 384
MLA_KV_RANK = 256
MLA_NOPE = 64
MLA_ROPE = 32
MLA_V = 64
MLA_QK = MLA_NOPE + MLA_ROPE
MLA_BLOCK = 128
MLA_SCALE = MLA_QK ** -0.5

PEER_HEADS = 8
PEER_N_KEYS = 128
PEER_EXPERTS = PEER_N_KEYS * PEER_N_KEYS
PEER_D_KEY = 256
PEER_HALF = PEER_D_KEY // 2
PEER_TOPK = 16
PEER_BLOCK = 64

IN_SIZES = (SSD_D_INNER, SSD_CONV_CH, SSD_HEADS, SSD_HEADS,
            SWA_HEADS * SWA_HEAD_DIM, SWA_KV_HEADS * SWA_HEAD_DIM, SWA_KV_HEADS * SWA_HEAD_DIM,
            MLA_Q_RANK, MLA_KV_RANK, MLA_ROPE,
            D_MODEL, D_MODEL, D_MODEL)
IN_WIDTH = sum(IN_SIZES)

kernel_name = 'hybrid_ssd_swa_mla_peer_diffusion_block'


def split_points():
    pts, acc = [], 0
    for s in IN_SIZES[:-1]:
        acc += s
        pts.append(acc)
    return pts


def rms_norm(x, g):
    xf = x.astype(jnp.float32)
    y = xf * lax.rsqrt(jnp.mean(xf * xf, axis=-1, keepdims=True) + EPS)
    return (y * g.astype(jnp.float32)).astype(x.dtype)


def axial_rope(rows, dim):
    pairs = dim // 4
    freqs = ROPE_BASE ** (-jnp.arange(pairs, dtype=jnp.float32) / pairs)
    pos_r = jnp.repeat(jnp.arange(rows, dtype=jnp.float32), GRID_W)
    pos_c = jnp.tile(jnp.arange(GRID_W, dtype=jnp.float32), rows)
    ang = jnp.concatenate([pos_r[:, None] * freqs, pos_c[:, None] * freqs], axis=-1)
    return jnp.cos(ang), jnp.sin(ang)


def apply_rope(x, cos, sin):
    d2 = x.shape[-1] // 2
    x1, x2 = x[..., :d2], x[..., d2:]
    cs, sn = cos[:, None, :], sin[:, None, :]
    return jnp.concatenate([x1 * cs - x2 * sn, x2 * cs + x1 * sn], axis=-1).astype(x.dtype)


def dwconv_centred(u, w, b):
    pad = (SSD_CONV - 1) // 2
    y = lax.conv_general_dilated(u, w[:, None, :].astype(u.dtype), window_strides=(1,),
                                 padding=[(pad, pad)], dimension_numbers=('NWC', 'WIO', 'NWC'),
                                 feature_group_count=u.shape[-1])
    return y + b


def ssd_scan(xh, dt, bm, cm, a, d_skip, init_state, with_y):
    f32 = jnp.float32
    bsz, T, G, Hg, P = xh.shape
    N = bm.shape[-1]
    Q = SSD_CHUNK
    nc = T // Q
    x_c = xh.reshape(bsz, nc, Q, G, Hg, P)
    dt_c = dt.astype(f32).reshape(bsz, nc, Q, G, Hg)
    b_c = bm.reshape(bsz, nc, Q, G, N)
    c_c = cm.reshape(bsz, nc, Q, G, N)
    cum = jnp.cumsum(dt_c * a, axis=2)
    decay_to_end = jnp.exp(cum[:, :, -1:] - cum)
    states = jnp.einsum('bcjgn,bcjgh,bcjghp->bcghpn', b_c, decay_to_end * dt_c, x_c)
    chunk_decay = jnp.exp(cum[:, :, -1])

    def step(s, inp):
        st, dec = inp
        return s * dec[..., None, None] + st, s

    final, prev = lax.scan(step, init_state.astype(f32),
                           (jnp.moveaxis(states, 1, 0), jnp.moveaxis(chunk_decay, 1, 0)))
    if not with_y:
        return None, final
    prev = jnp.moveaxis(prev, 0, 1)
    cum_t = jnp.moveaxis(cum, 2, -1)
    seg = cum_t[..., :, None] - cum_t[..., None, :]
    mask = jnp.tril(jnp.ones((Q, Q), dtype=bool))
    decay = jnp.where(mask, jnp.exp(jnp.where(mask, seg, 0.0)), 0.0)
    cb = jnp.einsum('bcign,bcjgn->bcgij', c_c, b_c)
    y_diag = jnp.einsum('bcgij,bcghij,bcjgh,bcjghp->bcighp', cb, decay, dt_c, x_c)
    y_off = jnp.einsum('bcign,bcghpn,bcigh->bcighp', c_c, prev, jnp.exp(cum))
    y = y_diag + y_off + d_skip[:, :, None] * x_c
    return y.reshape(bsz, T, G, Hg, P), final


def ssd_direction(ctx_in, lat_in, a, d_skip, reverse, with_ctx_out):
    if reverse:
        ctx_in = [jnp.flip(t, 1) for t in ctx_in]
        lat_in = [jnp.flip(t, 1) for t in lat_in]
    s0 = jnp.zeros((lat_in[0].shape[0], SSD_GROUPS, SSD_HEADS_PER_GROUP, SSD_HEAD_DIM, SSD_STATE), jnp.float32)
    y_c, s_c = ssd_scan(*ctx_in, a, d_skip, s0, with_ctx_out)
    y_l, _ = ssd_scan(*lat_in, a, d_skip, s_c, True)
    if reverse:
        y_l = jnp.flip(y_l, 1)
        if with_ctx_out:
            y_c = jnp.flip(y_c, 1)
    return y_l, y_c


def ssd_mixer(lat, ctx, p, with_ctx_out):
    f32 = jnp.float32

    def prep(z, xbc, dtf, dtb):
        u = jax.nn.silu(dwconv_centred(xbc, p['ssd_conv_w'], p['ssd_conv_b']))
        bsz, T, _ = u.shape
        xs, bm, cm = jnp.split(u, [SSD_D_INNER, SSD_D_INNER + SSD_GROUPS * SSD_STATE], axis=-1)
        xs = xs.reshape(bsz, T, SSD_GROUPS, SSD_HEADS_PER_GROUP, SSD_HEAD_DIM)
        bm = bm.reshape(bsz, T, SSD_GROUPS, SSD_STATE)
        cm = cm.reshape(bsz, T, SSD_GROUPS, SSD_STATE)
        dts = [jax.nn.softplus(d.astype(f32) + p['ssd_dt_bias'][i].astype(f32)).reshape(
            bsz, T, SSD_GROUPS, SSD_HEADS_PER_GROUP) for i, d in enumerate((dtf, dtb))]
        return xs, bm, cm, dts

    xl, bl, cl, dtl = prep(*lat)
    xc, bc, cc, dtc = prep(*ctx)
    ys_l, ys_c = [], []
    for direction in range(2):
        a = -jnp.exp(p['ssd_a_log'][direction].astype(f32)).reshape(SSD_GROUPS, SSD_HEADS_PER_GROUP)
        dsk = p['ssd_d'][direction].astype(f32).reshape(SSD_GROUPS, SSD_HEADS_PER_GROUP)
        y_l, y_c = ssd_direction((xc, dtc[direction], bc, cc), (xl, dtl[direction], bl, cl),
                                 a, dsk, direction == 1, with_ctx_out)
        ys_l.append(y_l)
        ys_c.append(y_c)

    def gated_out(y, z):
        y = y.reshape(z.shape[0], z.shape[1], SSD_D_INNER)
        return rms_norm(y * jax.nn.silu(z.astype(f32)), p['ssd_norm_g']).astype(z.dtype)

    out_l = gated_out(ys_l[0] + ys_l[1], lat[0])
    out_c = gated_out(ys_c[0] + ys_c[1], ctx[0]) if with_ctx_out else None
    return out_l, out_c


def swa_mixer(lat, ctx, p, rope, with_ctx_out):
    f32 = jnp.float32
    grp = SWA_HEADS // SWA_KV_HEADS

    def heads(q, k, v):
        bsz, T, _ = q.shape
        q = rms_norm(q.reshape(bsz, T, SWA_HEADS, SWA_HEAD_DIM), p['swa_q_norm'])
        k = rms_norm(k.reshape(bsz, T, SWA_KV_HEADS, SWA_HEAD_DIM), p['swa_k_norm'])
        return q, k, v.reshape(bsz, T, SWA_KV_HEADS, SWA_HEAD_DIM)

    ql, kl, vl = heads(*lat)
    qc, kc, vc = heads(*ctx)
    cos, sin = rope
    ql = apply_rope(ql, cos, sin) * SWA_SCALE
    kl = apply_rope(kl, cos, sin)
    bsz, L = ql.shape[:2]
    nb = L // SWA_BLOCK
    sink = p['swa_sink'].astype(f32).reshape(SWA_KV_HEADS, grp)

    qb = ql.reshape(bsz, nb, SWA_BLOCK, SWA_KV_HEADS, grp, SWA_HEAD_DIM)

    def band(t):
        t = t.reshape(bsz, nb, SWA_BLOCK, SWA_KV_HEADS, SWA_HEAD_DIM)
        tp = jnp.pad(t, ((0, 0), (1, 1), (0, 0), (0, 0), (0, 0)))
        return jnp.concatenate([tp[:, :-2], tp[:, 1:-1], tp[:, 2:]], axis=2)

    kb, vb = band(kl), band(vl)
    s_band = jnp.einsum('bnqkgd,bnjkd->bnkgqj', qb, kb).astype(f32)
    s_ctx = jnp.einsum('bnqkgd,bjkd->bnkgqj', qb, kc).astype(f32)
    qi = jnp.arange(SWA_BLOCK)[None, :, None]
    kj = jnp.arange(3 * SWA_BLOCK)[None, None, :]
    kpos = jnp.arange(nb)[:, None, None] * SWA_BLOCK + kj - SWA_BLOCK
    valid = (jnp.abs(kj - SWA_BLOCK - qi) <= SWA_WINDOW) & (kpos >= 0) & (kpos < L)
    s_band = jnp.where(valid[None, :, None, None], s_band, NEG_INF)
    sink_col = jnp.broadcast_to(sink[None, None, :, :, None, None], s_band.shape[:-1] + (1,))
    prob = jax.nn.softmax(jnp.concatenate([s_band, s_ctx, sink_col], axis=-1), axis=-1).astype(vl.dtype)
    nband = 3 * SWA_BLOCK
    o = (jnp.einsum('bnkgqj,bnjkd->bnqkgd', prob[..., :nband], vb)
         + jnp.einsum('bnkgqj,bjkd->bnqkgd', prob[..., nband:-1], vc))
    y_l = o.reshape(bsz, L, SWA_HEADS * SWA_HEAD_DIM)

    y_c = None
    if with_ctx_out:
        qcb = (qc * SWA_SCALE).reshape(bsz, -1, SWA_KV_HEADS, grp, SWA_HEAD_DIM)
        s = jnp.einsum('bqkgd,bjkd->bkgqj', qcb, kc).astype(f32)
        sink_c = jnp.broadcast_to(sink[None, :, :, None, None], s.shape[:-1] + (1,))
        pc = jax.nn.softmax(jnp.concatenate([s, sink_c], axis=-1), axis=-1)[..., :-1].astype(vc.dtype)
        y_c = jnp.einsum('bkgqj,bjkd->bqkgd', pc, vc).reshape(bsz, -1, SWA_HEADS * SWA_HEAD_DIM)
    return y_l, y_c


def mla_mixer(lat, ctx, p, rope, with_ctx_out):
    f32 = jnp.float32

    def heads(cq, ckv, kr):
        bsz, T, _ = cq.shape
        q = (rms_norm(cq, p['mla_q_a_norm']) @ p['mla_w_uq']).reshape(bsz, T, MLA_HEADS, MLA_QK)
        kv = (rms_norm(ckv, p['mla_kv_a_norm']) @ p['mla_w_ukv']).reshape(bsz, T, MLA_HEADS, MLA_NOPE + MLA_V)
        k = jnp.concatenate([kv[..., :MLA_NOPE],
                             jnp.broadcast_to(kr[:, :, None, :], (bsz, T, MLA_HEADS, MLA_ROPE))], axis=-1)
        return rms_norm(q, p['mla_q_norm']), rms_norm(k, p['mla_k_norm']), kv[..., MLA_NOPE:]

    cos, sin = rope

    def rope_tail(t):
        return jnp.concatenate([t[..., :MLA_NOPE], apply_rope(t[..., MLA_NOPE:], cos, sin)], axis=-1)

    ql, kl, vl = heads(*lat)
    qc, kc, vc = heads(*ctx)
    ql = rope_tail(ql) * MLA_SCALE
    kl = rope_tail(kl)
    bsz, L = ql.shape[:2]
    nb = L // MLA_BLOCK
    k_all = jnp.concatenate([kc, kl], axis=1)
    v_all = jnp.concatenate([vc, vl], axis=1)

    def attend(qb, k, v):
        s = jnp.einsum('bqhd,bkhd->bhqk', qb, k).astype(f32)
        pr = jax.nn.softmax(s, axis=-1).astype(v.dtype)
        return jnp.einsum('bhqk,bkhd->bqhd', pr, v)

    q_blocks = jnp.swapaxes(ql.reshape(bsz, nb, MLA_BLOCK, MLA_HEADS, MLA_QK), 0, 1)
    o = lax.map(lambda qb: attend(qb, k_all, v_all), q_blocks)
    y_l = jnp.swapaxes(o, 0, 1).reshape(bsz, L, MLA_HEADS * MLA_V)
    y_c = None
    if with_ctx_out:
        y_c = attend(qc * MLA_SCALE, kc, vc).reshape(bsz, -1, MLA_HEADS * MLA_V)
    return y_l, y_c


def token_mixers(h_l, h_c, p, rope_swa, rope_mla, with_ctx_out):
    pts = split_points()
    pl = jnp.split(h_l @ p['w_in'], pts, axis=-1)
    pc = jnp.split(h_c @ p['w_in'], pts, axis=-1)
    y_ssd_l, y_ssd_c = ssd_mixer(pl[0:4], pc[0:4], p, with_ctx_out)
    y_swa_l, y_swa_c = swa_mixer(pl[4:7], pc[4:7], p, rope_swa, with_ctx_out)
    y_mla_l, y_mla_c = mla_mixer(pl[7:10], pc[7:10], p, rope_mla, with_ctx_out)
    b_ga, b_gb, b_gc = jnp.split(p['b_gate'], 3)

    def merge(ys, yw, ym, gates):
        ga = jax.nn.sigmoid(gates[0] + b_ga)
        gb = jax.nn.sigmoid(gates[1] + b_gb)
        gc = jax.nn.sigmoid(gates[2] + b_gc)
        m = ga * (ys @ p['w_br_ssd']) + gb * (yw @ p['w_br_swa']) + gc * (ym @ p['w_br_mla'])
        return m @ p['w_out']

    out_l = merge(y_ssd_l, y_swa_l, y_mla_l, pl[10:13])
    out_c = merge(y_ssd_c, y_swa_c, y_mla_c, pc[10:13]) if with_ctx_out else None
    return out_l, out_c


def peer_ffn(h, w_q, sub_keys, u_tab, v_tab):
    bsz, T, D = h.shape
    q = (h @ w_q).reshape(bsz, T, PEER_HEADS, 2, PEER_HALF)
    s = jnp.einsum('bthpd,phkd->bthpk', q, sub_keys).astype(jnp.float32)
    top_s, top_i = lax.top_k(s, PEER_TOPK)
    n_cand = PEER_TOPK * PEER_TOPK
    cand_s = (top_s[..., 0, :, None] + top_s[..., 1, None, :]).reshape(bsz, T, PEER_HEADS, n_cand)
    cand_i = (top_i[..., 0, :, None] * PEER_N_KEYS + top_i[..., 1, None, :]).reshape(bsz, T, PEER_HEADS, n_cand)
    best_s, pos = lax.top_k(cand_s, PEER_TOPK)
    idx = jnp.take_along_axis(cand_i, pos, axis=-1)
    gate = jax.nn.softmax(best_s, axis=-1).astype(h.dtype)
    nb = T // PEER_BLOCK

    def to_blocks(t):
        return jnp.swapaxes(t.reshape((bsz, nb, PEER_BLOCK) + t.shape[2:]), 0, 1)

    def expert_block(args):
        hb, ib, gb = args
        act = jax.nn.gelu(jnp.einsum('btd,bthkd->bthk', hb, u_tab[ib]), approximate=False)
        return jnp.einsum('bthk,bthkd->btd', gb * act, v_tab[ib])

    out = lax.map(expert_block, (to_blocks(h), to_blocks(idx), to_blocks(gate)))
    return jnp.swapaxes(out, 0, 1).reshape(bsz, T, D)


def layer(xl, xc, c, c_ctx, p, rope_swa, rope_mla, with_ctx_out):
    mod_l = (jax.nn.silu(c) @ p['w_mod'] + p['b_mod'])[:, None, :]
    mod_c = (jax.nn.silu(c_ctx) @ p['w_mod'] + p['b_mod'])[None, None, :]
    sh1_l, sc1_l, g1_l, sh2_l, sc2_l, g2_l = jnp.split(mod_l, 6, axis=-1)
    sh1_c, sc1_c, g1_c, sh2_c, sc2_c, g2_c = jnp.split(mod_c, 6, axis=-1)
    h_l = rms_norm(xl, p['norm1_g']) * (1 + sc1_l) + sh1_l
    h_c = rms_norm(xc, p['norm1_g']) * (1 + sc1_c) + sh1_c
    mix_l, mix_c = token_mixers(h_l, h_c, p, rope_swa, rope_mla, with_ctx_out)
    xl = xl + g1_l * mix_l
    h2_l = rms_norm(xl, p['norm2_g']) * (1 + sc2_l) + sh2_l
    if with_ctx_out:
        xc = xc + g1_c * mix_c
        h2_c = rms_norm(xc, p['norm2_g']) * (1 + sc2_c) + sh2_c
        n_ctx = xc.shape[1]
        f = peer_ffn(jnp.concatenate([h2_c, h2_l], axis=1), p['peer_w_q'], p['peer_keys'], p['peer_u'], p['peer_v'])
        f_c, f_l = f[:, :n_ctx], f[:, n_ctx:]
        xc = xc + g2_c * f_c
    else:
        f_l = peer_ffn(h2_l, p['peer_w_q'], p['peer_keys'], p['peer_u'], p['peer_v'])
    xl = xl + g2_l * f_l
    return xl, xc


def setup_inputs(seed: int = 0) -> dict:
    key = jax.random.key(seed)
    ks = iter(jax.random.split(key, 40))
    D = D_MODEL

    def nrm(shape, scale):
        return jax.random.normal(next(ks), shape, jnp.float32) * scale

    def gain(shape):
        return 1.0 + nrm(shape, 0.02)

    x = nrm((BATCH, SEQ, D), 1.0)
    c = nrm((BATCH, D), 1.0)
    ctx = nrm((BATCH, CTX_LEN, D), 1.0)
    c_ctx = nrm((D,), 1.0)
    w_mod = nrm((DEPTH, D, 6 * D), 0.5 * D ** -0.5)
    b_mod = nrm((DEPTH, 6 * D), 0.01)
    norm1_g = gain((DEPTH, D))
    norm2_g = gain((DEPTH, D))
    w_in = nrm((DEPTH, D, IN_WIDTH), D ** -0.5)
    ssd_conv_w = nrm((DEPTH, SSD_CONV, SSD_CONV_CH), SSD_CONV ** -0.5)
    ssd_conv_b = nrm((DEPTH, SSD_CONV_CH), 0.01)
    dt0 = jnp.exp(jax.random.uniform(next(ks), (DEPTH, 2, SSD_HEADS), jnp.float32,
                                     minval=float(np.log(1e-3)), maxval=float(np.log(1e-1))))
    ssd_dt_bias = dt0 + jnp.log(-jnp.expm1(-dt0))
    ssd_a_log = jnp.log(jax.random.uniform(next(ks), (DEPTH, 2, SSD_HEADS), jnp.float32, minval=1.0, maxval=16.0))
    ssd_d = gain((DEPTH, 2, SSD_HEADS))
    ssd_norm_g = gain((DEPTH, SSD_D_INNER))
    swa_q_norm = gain((DEPTH, SWA_HEAD_DIM))
    swa_k_norm = gain((DEPTH, SWA_HEAD_DIM))
    swa_sink = nrm((DEPTH, SWA_HEADS), 0.5)
    mla_q_a_norm = gain((DEPTH, MLA_Q_RANK))
    mla_kv_a_norm = gain((DEPTH, MLA_KV_RANK))
    mla_w_uq = nrm((DEPTH, MLA_Q_RANK, MLA_HEADS * MLA_QK), MLA_Q_RANK ** -0.5)
    mla_w_ukv = nrm((DEPTH, MLA_KV_RANK, MLA_HEADS * (MLA_NOPE + MLA_V)), MLA_KV_RANK ** -0.5)
    mla_q_norm = gain((DEPTH, MLA_QK))
    mla_k_norm = gain((DEPTH, MLA_QK))
    b_gate = nrm((DEPTH, 3 * D), 0.01)
    w_br_ssd = nrm((DEPTH, SSD_D_INNER, D), SSD_D_INNER ** -0.5)
    w_br_swa = nrm((DEPTH, SWA_HEADS * SWA_HEAD_DIM, D), (SWA_HEADS * SWA_HEAD_DIM) ** -0.5)
    w_br_mla = nrm((DEPTH, MLA_HEADS * MLA_V, D), (MLA_HEADS * MLA_V) ** -0.5)
    w_out = nrm((DEPTH, D, D), D ** -0.5)
    peer_w_q = nrm((DEPTH, D, PEER_HEADS * PEER_D_KEY), D ** -0.5)
    peer_keys = nrm((DEPTH, 2, PEER_HEADS, PEER_N_KEYS, PEER_HALF), PEER_HALF ** -0.5)
    peer_u = nrm((DEPTH, PEER_EXPERTS, D), D ** -0.5)
    peer_v = nrm((DEPTH, PEER_EXPERTS, D), PEER_HEADS ** -0.5)
    return {'x': x, 'c': c, 'ctx': ctx, 'c_ctx': c_ctx, 'w_mod': w_mod, 'b_mod': b_mod,
            'norm1_g': norm1_g, 'norm2_g': norm2_g, 'w_in': w_in, 'ssd_conv_w': ssd_conv_w,
            'ssd_conv_b': ssd_conv_b, 'ssd_dt_bias': ssd_dt_bias, 'ssd_a_log': ssd_a_log, 'ssd_d': ssd_d,
            'ssd_norm_g': ssd_norm_g, 'swa_q_norm': swa_q_norm, 'swa_k_norm': swa_k_norm, 'swa_sink': swa_sink,
            'mla_q_a_norm': mla_q_a_norm, 'mla_kv_a_norm': mla_kv_a_norm, 'mla_w_uq': mla_w_uq,
            'mla_w_ukv': mla_w_ukv, 'mla_q_norm': mla_q_norm, 'mla_k_norm': mla_k_norm, 'b_gate': b_gate,
            'w_br_ssd': w_br_ssd, 'w_br_swa': w_br_swa, 'w_br_mla': w_br_mla, 'w_out': w_out,
            'peer_w_q': peer_w_q, 'peer_keys': peer_keys, 'peer_u': peer_u, 'peer_v': peer_v}


def reference(x, c, ctx, c_ctx, w_mod, b_mod, norm1_g, norm2_g, w_in, ssd_conv_w, ssd_conv_b, ssd_dt_bias,
              ssd_a_log, ssd_d, ssd_norm_g, swa_q_norm, swa_k_norm, swa_sink, mla_q_a_norm, mla_kv_a_norm,
              mla_w_uq, mla_w_ukv, mla_q_norm, mla_k_norm, b_gate, w_br_ssd, w_br_swa, w_br_mla, w_out,
              peer_w_q, peer_keys, peer_u, peer_v):
    L = x.shape[1]
    rows = L // GRID_W
    rope_swa = axial_rope(rows, SWA_HEAD_DIM)
    rope_mla = axial_rope(rows, MLA_ROPE)
    xl, xc = x, ctx
    for i in range(DEPTH):
        p = {'w_mod': w_mod[i], 'b_mod': b_mod[i], 'norm1_g': norm1_g[i], 'norm2_g': norm2_g[i],
             'w_in': w_in[i], 'ssd_conv_w': ssd_conv_w[i], 'ssd_conv_b': ssd_conv_b[i],
             'ssd_dt_bias': ssd_dt_bias[i], 'ssd_a_log': ssd_a_log[i], 'ssd_d': ssd_d[i],
             'ssd_norm_g': ssd_norm_g[i], 'swa_q_norm': swa_q_norm[i], 'swa_k_norm': swa_k_norm[i],
             'swa_sink': swa_sink[i], 'mla_q_a_norm': mla_q_a_norm[i], 'mla_kv_a_norm': mla_kv_a_norm[i],
             'mla_w_uq': mla_w_uq[i], 'mla_w_ukv': mla_w_ukv[i], 'mla_q_norm': mla_q_norm[i],
             'mla_k_norm': mla_k_norm[i], 'b_gate': b_gate[i], 'w_br_ssd': w_br_ssd[i], 'w_br_swa': w_br_swa[i],
             'w_br_mla': w_br_mla[i], 'w_out': w_out[i], 'peer_w_q': peer_w_q[i], 'peer_keys': peer_keys[i],
             'peer_u': peer_u[i], 'peer_v': peer_v[i]}
        xl, xc = layer(xl, xc, c, c_ctx, p, rope_swa, rope_mla, i < DEPTH - 1)
    return xl
```

```python
import functools

import jax
import jax.numpy as jnp
from jax import lax
from jax.experimental import pallas as pl
from jax.experimental.pallas import tpu as pltpu

D_MODEL = 1024
DEPTH = 2
GRID_W = 64
EPS = 1e-6
NEG_INF = -1e30
ROPE_BASE = 10000.0

SSD_D_INNER = 1024
SSD_HEAD_DIM = 64
SSD_HEADS = 16
SSD_GROUPS = 2
SSD_HEADS_PER_GROUP = 8
SSD_STATE = 64
SSD_CONV = 5
SSD_CHUNK = 128
SSD_CONV_CH = 1280

SWA_HEADS = 8
SWA_KV_HEADS = 2
SWA_HEAD_DIM = 64
SWA_WINDOW = 128
SWA_BLOCK = 128
SWA_SCALE = SWA_HEAD_DIM ** -0.5

MLA_HEADS = 8
MLA_Q_RANK = 384
MLA_KV_RANK = 256
MLA_NOPE = 64
MLA_ROPE = 32
MLA_V = 64
MLA_QK = MLA_NOPE + MLA_ROPE
MLA_BLOCK = 128
MLA_SCALE = MLA_QK ** -0.5

PEER_HEADS = 8
PEER_N_KEYS = 128
PEER_D_KEY = 256
PEER_HALF = 128
PEER_TOPK = 16
PEER_BLOCK = 64

IN_SIZES = (SSD_D_INNER, SSD_CONV_CH, SSD_HEADS, SSD_HEADS,
            SWA_HEADS * SWA_HEAD_DIM, SWA_KV_HEADS * SWA_HEAD_DIM, SWA_KV_HEADS * SWA_HEAD_DIM,
            MLA_Q_RANK, MLA_KV_RANK, MLA_ROPE,
            D_MODEL, D_MODEL, D_MODEL)
IN_WIDTH = sum(IN_SIZES)

LANES = 128


def _round_up(n, m):
    return (n + m - 1) // m * m


def _mm_kernel(a_ref, b_ref, o_ref):
    o_ref[...] = jnp.dot(a_ref[...].astype(jnp.bfloat16), b_ref[...],
                         preferred_element_type=jnp.float32)


def _pick_tile(n, cap, unit):
    t = min(n, cap)
    while n % t or t % unit:
        t -= unit
    return t


def pmatmul(a, w):
    m, k = a.shape
    n = w.shape[1]
    n_pad = _round_up(n, LANES)
    wb = w.astype(jnp.bfloat16)
    if n_pad != n:
        wb = jnp.pad(wb, ((0, 0), (0, n_pad - n)))
    tm = _pick_tile(m, 512, 8)
    tn = _pick_tile(n_pad, 1024, LANES)
    out = pl.pallas_call(
        _mm_kernel,
        grid=(m // tm, n_pad // tn),
        in_specs=[pl.BlockSpec((tm, k), lambda i, j: (i, 0)),
                  pl.BlockSpec((k, tn), lambda i, j: (0, j))],
        out_specs=pl.BlockSpec((tm, tn), lambda i, j: (i, j)),
        out_shape=jax.ShapeDtypeStruct((m, n_pad), jnp.float32),
        compiler_params=pltpu.CompilerParams(dimension_semantics=("parallel", "parallel")),
        name="matmul",
    )(a, wb)
    return out[:, :n] if n_pad != n else out


def mm(a, w):
    lead = a.shape[:-1]
    return pmatmul(a.reshape(-1, a.shape[-1]), w).reshape(lead + (w.shape[1],))


def split_points():
    pts, acc = [], 0
    for s in IN_SIZES[:-1]:
        acc += s
        pts.append(acc)
    return pts


def rms_norm(x, g):
    xf = x.astype(jnp.float32)
    y = xf * lax.rsqrt(jnp.mean(xf * xf, axis=-1, keepdims=True) + EPS)
    return (y * g.astype(jnp.float32)).astype(x.dtype)


def axial_rope(rows, dim):
    pairs = dim // 4
    freqs = ROPE_BASE ** (-jnp.arange(pairs, dtype=jnp.float32) / pairs)
    pos_r = jnp.repeat(jnp.arange(rows, dtype=jnp.float32), GRID_W)
    pos_c = jnp.tile(jnp.arange(GRID_W, dtype=jnp.float32), rows)
    ang = jnp.concatenate([pos_r[:, None] * freqs, pos_c[:, None] * freqs], axis=-1)
    return jnp.cos(ang), jnp.sin(ang)


def apply_rope(x, cos, sin):
    d2 = x.shape[-1] // 2
    x1, x2 = x[..., :d2], x[..., d2:]
    cs, sn = cos[:, None, :], sin[:, None, :]
    return jnp.concatenate([x1 * cs - x2 * sn, x2 * cs + x1 * sn], axis=-1).astype(x.dtype)


def dwconv_centred(u, w, b):
    pad = (SSD_CONV - 1) // 2
    y = lax.conv_general_dilated(u, w[:, None, :].astype(u.dtype), window_strides=(1,),
                                 padding=[(pad, pad)], dimension_numbers=('NWC', 'WIO', 'NWC'),
                                 feature_group_count=u.shape[-1])
    return y + b


def ssd_scan(xh, dt, bm, cm, a, d_skip, init_state, with_y):
    f32 = jnp.float32
    bsz, T, G, Hg, P = xh.shape
    N = bm.shape[-1]
    Q = SSD_CHUNK
    nc = T // Q
    x_c = xh.reshape(bsz, nc, Q, G, Hg, P)
    dt_c = dt.astype(f32).reshape(bsz, nc, Q, G, Hg)
    b_c = bm.reshape(bsz, nc, Q, G, N)
    c_c = cm.reshape(bsz, nc, Q, G, N)
    cum = jnp.cumsum(dt_c * a, axis=2)
    decay_to_end = jnp.exp(cum[:, :, -1:] - cum)
    states = jnp.einsum('bcjgn,bcjgh,bcjghp->bcghpn', b_c, decay_to_end * dt_c, x_c)
    chunk_decay = jnp.exp(cum[:, :, -1])

    def step(s, inp):
        st, dec = inp
        return s * dec[..., None, None] + st, s

    final, prev = lax.scan(step, init_state.astype(f32),
                           (jnp.moveaxis(states, 1, 0), jnp.moveaxis(chunk_decay, 1, 0)))
    if not with_y:
        return None, final
    prev = jnp.moveaxis(prev, 0, 1)
    cum_t = jnp.moveaxis(cum, 2, -1)
    seg = cum_t[..., :, None] - cum_t[..., None, :]
    mask = jnp.tril(jnp.ones((Q, Q), dtype=bool))
    decay = jnp.where(mask, jnp.exp(jnp.where(mask, seg, 0.0)), 0.0)
    cb = jnp.einsum('bcign,bcjgn->bcgij', c_c, b_c)
    y_diag = jnp.einsum('bcgij,bcghij,bcjgh,bcjghp->bcighp', cb, decay, dt_c, x_c)
    y_off = jnp.einsum('bcign,bcghpn,bcigh->bcighp', c_c, prev, jnp.exp(cum))
    y = y_diag + y_off + d_skip[:, :, None] * x_c
    return y.reshape(bsz, T, G, Hg, P), final


def ssd_direction(ctx_in, lat_in, a, d_skip, reverse, with_ctx_out):
    if reverse:
        ctx_in = [jnp.flip(t, 1) for t in ctx_in]
        lat_in = [jnp.flip(t, 1) for t in lat_in]
    s0 = jnp.zeros((lat_in[0].shape[0], SSD_GROUPS, SSD_HEADS_PER_GROUP, SSD_HEAD_DIM, SSD_STATE), jnp.float32)
    y_c, s_c = ssd_scan(*ctx_in, a, d_skip, s0, with_ctx_out)
    y_l, _ = ssd_scan(*lat_in, a, d_skip, s_c, True)
    if reverse:
        y_l = jnp.flip(y_l, 1)
        if with_ctx_out:
            y_c = jnp.flip(y_c, 1)
    return y_l, y_c


def ssd_mixer(lat, ctx, p, with_ctx_out):
    f32 = jnp.float32

    def prep(z, xbc, dtf, dtb):
        u = jax.nn.silu(dwconv_centred(xbc, p['ssd_conv_w'], p['ssd_conv_b']))
        bsz, T, _ = u.shape
        xs, bm, cm = jnp.split(u, [SSD_D_INNER, SSD_D_INNER + SSD_GROUPS * SSD_STATE], axis=-1)
        xs = xs.reshape(bsz, T, SSD_GROUPS, SSD_HEADS_PER_GROUP, SSD_HEAD_DIM)
        bm = bm.reshape(bsz, T, SSD_GROUPS, SSD_STATE)
        cm = cm.reshape(bsz, T, SSD_GROUPS, SSD_STATE)
        dts = [jax.nn.softplus(d.astype(f32) + p['ssd_dt_bias'][i].astype(f32)).reshape(
            bsz, T, SSD_GROUPS, SSD_HEADS_PER_GROUP) for i, d in enumerate((dtf, dtb))]
        return xs, bm, cm, dts

    xl, bl, cl, dtl = prep(*lat)
    xc, bc, cc, dtc = prep(*ctx)
    ys_l, ys_c = [], []
    for direction in range(2):
        a = -jnp.exp(p['ssd_a_log'][direction].astype(f32)).reshape(SSD_GROUPS, SSD_HEADS_PER_GROUP)
        dsk = p['ssd_d'][direction].astype(f32).reshape(SSD_GROUPS, SSD_HEADS_PER_GROUP)
        y_l, y_c = ssd_direction((xc, dtc[direction], bc, cc), (xl, dtl[direction], bl, cl),
                                 a, dsk, direction == 1, with_ctx_out)
        ys_l.append(y_l)
        ys_c.append(y_c)

    def gated_out(y, z):
        y = y.reshape(z.shape[0], z.shape[1], SSD_D_INNER)
        return rms_norm(y * jax.nn.silu(z.astype(f32)), p['ssd_norm_g']).astype(z.dtype)

    out_l = gated_out(ys_l[0] + ys_l[1], lat[0])
    out_c = gated_out(ys_c[0] + ys_c[1], ctx[0]) if with_ctx_out else None
    return out_l, out_c


def swa_mixer(lat, ctx, p, rope, with_ctx_out):
    f32 = jnp.float32
    grp = SWA_HEADS // SWA_KV_HEADS

    def heads(q, k, v):
        bsz, T, _ = q.shape
        q = rms_norm(q.reshape(bsz, T, SWA_HEADS, SWA_HEAD_DIM), p['swa_q_norm'])
        k = rms_norm(k.reshape(bsz, T, SWA_KV_HEADS, SWA_HEAD_DIM), p['swa_k_norm'])
        return q, k, v.reshape(bsz, T, SWA_KV_HEADS, SWA_HEAD_DIM)

    ql, kl, vl = heads(*lat)
    qc, kc, vc = heads(*ctx)
    cos, sin = rope
    ql = apply_rope(ql, cos, sin) * SWA_SCALE
    kl = apply_rope(kl, cos, sin)
    bsz, L = ql.shape[:2]
    nb = L // SWA_BLOCK
    sink = p['swa_sink'].astype(f32).reshape(SWA_KV_HEADS, grp)
    qb = ql.reshape(bsz, nb, SWA_BLOCK, SWA_KV_HEADS, grp, SWA_HEAD_DIM)

    def band(t):
        t = t.reshape(bsz, nb, SWA_BLOCK, SWA_KV_HEADS, SWA_HEAD_DIM)
        tp = jnp.pad(t, ((0, 0), (1, 1), (0, 0), (0, 0), (0, 0)))
        return jnp.concatenate([tp[:, :-2], tp[:, 1:-1], tp[:, 2:]], axis=2)

    kb, vb = band(kl), band(vl)
    s_band = jnp.einsum('bnqkgd,bnjkd->bnkgqj', qb, kb).astype(f32)
    s_ctx = jnp.einsum('bnqkgd,bjkd->bnkgqj', qb, kc).astype(f32)
    qi = jnp.arange(SWA_BLOCK)[None, :, None]
    kj = jnp.arange(3 * SWA_BLOCK)[None, None, :]
    kpos = jnp.arange(nb)[:, None, None] * SWA_BLOCK + kj - SWA_BLOCK
    valid = (jnp.abs(kj - SWA_BLOCK - qi) <= SWA_WINDOW) & (kpos >= 0) & (kpos < L)
    s_band = jnp.where(valid[None, :, None, None], s_band, NEG_INF)
    sink_col = jnp.broadcast_to(sink[None, None, :, :, None, None], s_band.shape[:-1] + (1,))
    prob = jax.nn.softmax(jnp.concatenate([s_band, s_ctx, sink_col], axis=-1), axis=-1).astype(vl.dtype)
    nband = 3 * SWA_BLOCK
    o = (jnp.einsum('bnkgqj,bnjkd->bnqkgd', prob[..., :nband], vb)
         + jnp.einsum('bnkgqj,bjkd->bnqkgd', prob[..., nband:-1], vc))
    y_l = o.reshape(bsz, L, SWA_HEADS * SWA_HEAD_DIM)

    y_c = None
    if with_ctx_out:
        qcb = (qc * SWA_SCALE).reshape(bsz, -1, SWA_KV_HEADS, grp, SWA_HEAD_DIM)
        s = jnp.einsum('bqkgd,bjkd->bkgqj', qcb, kc).astype(f32)
        sink_c = jnp.broadcast_to(sink[None, :, :, None, None], s.shape[:-1] + (1,))
        pc = jax.nn.softmax(jnp.concatenate([s, sink_c], axis=-1), axis=-1)[..., :-1].astype(vc.dtype)
        y_c = jnp.einsum('bkgqj,bjkd->bqkgd', pc, vc).reshape(bsz, -1, SWA_HEADS * SWA_HEAD_DIM)
    return y_l, y_c


def mla_mixer(lat, ctx, p, rope, with_ctx_out):
    f32 = jnp.float32

    def heads(cq, ckv, kr):
        bsz, T, _ = cq.shape
        q = mm(rms_norm(cq, p['mla_q_a_norm']), p['mla_w_uq']).reshape(bsz, T, MLA_HEADS, MLA_QK)
        kv = mm(rms_norm(ckv, p['mla_kv_a_norm']), p['mla_w_ukv']).reshape(bsz, T, MLA_HEADS, MLA_NOPE + MLA_V)
        k = jnp.concatenate([kv[..., :MLA_NOPE],
                             jnp.broadcast_to(kr[:, :, None, :], (bsz, T, MLA_HEADS, MLA_ROPE))], axis=-1)
        return rms_norm(q, p['mla_q_norm']), rms_norm(k, p['mla_k_norm']), kv[..., MLA_NOPE:]

    cos, sin = rope

    def rope_tail(t):
        return jnp.concatenate([t[..., :MLA_NOPE], apply_rope(t[..., MLA_NOPE:], cos, sin)], axis=-1)

    ql, kl, vl = heads(*lat)
    qc, kc, vc = heads(*ctx)
    ql = rope_tail(ql) * MLA_SCALE
    kl = rope_tail(kl)
    bsz, L = ql.shape[:2]
    nb = L // MLA_BLOCK
    k_all = jnp.concatenate([kc, kl], axis=1)
    v_all = jnp.concatenate([vc, vl], axis=1)

    def attend(qb, k, v):
        s = jnp.einsum('bqhd,bkhd->bhqk', qb, k).astype(f32)
        pr = jax.nn.softmax(s, axis=-1).astype(v.dtype)
        return jnp.einsum('bhqk,bkhd->bqhd', pr, v)

    q_blocks = jnp.swapaxes(ql.reshape(bsz, nb, MLA_BLOCK, MLA_HEADS, MLA_QK), 0, 1)
    o = lax.map(lambda qb: attend(qb, k_all, v_all), q_blocks)
    y_l = jnp.swapaxes(o, 0, 1).reshape(bsz, L, MLA_HEADS * MLA_V)
    y_c = None
    if with_ctx_out:
        y_c = attend(qc * MLA_SCALE, kc, vc).reshape(bsz, -1, MLA_HEADS * MLA_V)
    return y_l, y_c


def token_mixers(h_l, h_c, p, rope_swa, rope_mla, with_ctx_out):
    pts = split_points()
    pl_ = jnp.split(mm(h_l, p['w_in']), pts, axis=-1)
    pc = jnp.split(mm(h_c, p['w_in']), pts, axis=-1)
    y_ssd_l, y_ssd_c = ssd_mixer(pl_[0:4], pc[0:4], p, with_ctx_out)
    y_swa_l, y_swa_c = swa_mixer(pl_[4:7], pc[4:7], p, rope_swa, with_ctx_out)
    y_mla_l, y_mla_c = mla_mixer(pl_[7:10], pc[7:10], p, rope_mla, with_ctx_out)
    b_ga, b_gb, b_gc = jnp.split(p['b_gate'], 3)

    def merge(ys, yw, ym, gates):
        ga = jax.nn.sigmoid(gates[0] + b_ga)
        gb = jax.nn.sigmoid(gates[1] + b_gb)
        gc = jax.nn.sigmoid(gates[2] + b_gc)
        m = ga * mm(ys, p['w_br_ssd']) + gb * mm(yw, p['w_br_swa']) + gc * mm(ym, p['w_br_mla'])
        return mm(m, p['w_out'])

    out_l = merge(y_ssd_l, y_swa_l, y_mla_l, pl_[10:13])
    out_c = merge(y_ssd_c, y_swa_c, y_mla_c, pc[10:13]) if with_ctx_out else None
    return out_l, out_c


def peer_ffn(h, w_q, sub_keys, u_tab, v_tab):
    bsz, T, D = h.shape
    q = mm(h, w_q).reshape(bsz, T, PEER_HEADS, 2, PEER_HALF)
    s = jnp.einsum('bthpd,phkd->bthpk', q, sub_keys).astype(jnp.float32)
    top_s, top_i = lax.top_k(s, PEER_TOPK)
    n_cand = PEER_TOPK * PEER_TOPK
    cand_s = (top_s[..., 0, :, None] + top_s[..., 1, None, :]).reshape(bsz, T, PEER_HEADS, n_cand)
    cand_i = (top_i[..., 0, :, None] * PEER_N_KEYS + top_i[..., 1, None, :]).reshape(bsz, T, PEER_HEADS, n_cand)
    best_s, pos = lax.top_k(cand_s, PEER_TOPK)
    idx = jnp.take_along_axis(cand_i, pos, axis=-1)
    gate = jax.nn.softmax(best_s, axis=-1).astype(h.dtype)
    nb = T // PEER_BLOCK

    def to_blocks(t):
        return jnp.swapaxes(t.reshape((bsz, nb, PEER_BLOCK) + t.shape[2:]), 0, 1)

    def expert_block(args):
        hb, ib, gb = args
        act = jax.nn.gelu(jnp.einsum('btd,bthkd->bthk', hb, u_tab[ib]), approximate=False)
        return jnp.einsum('bthk,bthkd->btd', gb * act, v_tab[ib])

    out = lax.map(expert_block, (to_blocks(h), to_blocks(idx), to_blocks(gate)))
    return jnp.swapaxes(out, 0, 1).reshape(bsz, T, D)


def layer(xl, xc, c, c_ctx, p, rope_swa, rope_mla, with_ctx_out):
    mod_l = (jax.nn.silu(c) @ p['w_mod'] + p['b_mod'])[:, None, :]
    mod_c = (jax.nn.silu(c_ctx) @ p['w_mod'] + p['b_mod'])[None, None, :]
    sh1_l, sc1_l, g1_l, sh2_l, sc2_l, g2_l = jnp.split(mod_l, 6, axis=-1)
    sh1_c, sc1_c, g1_c, sh2_c, sc2_c, g2_c = jnp.split(mod_c, 6, axis=-1)
    h_l = rms_norm(xl, p['norm1_g']) * (1 + sc1_l) + sh1_l
    h_c = rms_norm(xc, p['norm1_g']) * (1 + sc1_c) + sh1_c
    mix_l, mix_c = token_mixers(h_l, h_c, p, rope_swa, rope_mla, with_ctx_out)
    xl = xl + g1_l * mix_l
    h2_l = rms_norm(xl, p['norm2_g']) * (1 + sc2_l) + sh2_l
    if with_ctx_out:
        xc = xc + g1_c * mix_c
        h2_c = rms_norm(xc, p['norm2_g']) * (1 + sc2_c) + sh2_c
        n_ctx = xc.shape[1]
        f = peer_ffn(jnp.concatenate([h2_c, h2_l], axis=1), p['peer_w_q'], p['peer_keys'], p['peer_u'], p['peer_v'])
        f_c, f_l = f[:, :n_ctx], f[:, n_ctx:]
        xc = xc + g2_c * f_c
    else:
        f_l = peer_ffn(h2_l, p['peer_w_q'], p['peer_keys'], p['peer_u'], p['peer_v'])
    xl = xl + g2_l * f_l
    return xl, xc


def kernel(x, c, ctx, c_ctx, w_mod, b_mod, norm1_g, norm2_g, w_in, ssd_conv_w, ssd_conv_b, ssd_dt_bias,
           ssd_a_log, ssd_d, ssd_norm_g, swa_q_norm, swa_k_norm, swa_sink, mla_q_a_norm, mla_kv_a_norm,
           mla_w_uq, mla_w_ukv, mla_q_norm, mla_k_norm, b_gate, w_br_ssd, w_br_swa, w_br_mla, w_out,
           peer_w_q, peer_keys, peer_u, peer_v):
    L = x.shape[1]
    rows = L // GRID_W
    rope_swa = axial_rope(rows, SWA_HEAD_DIM)
    rope_mla = axial_rope(rows, MLA_ROPE)
    params = dict(w_mod=w_mod, b_mod=b_mod, norm1_g=norm1_g, norm2_g=norm2_g, w_in=w_in,
                  ssd_conv_w=ssd_conv_w, ssd_conv_b=ssd_conv_b, ssd_dt_bias=ssd_dt_bias,
                  ssd_a_log=ssd_a_log, ssd_d=ssd_d, ssd_norm_g=ssd_norm_g, swa_q_norm=swa_q_norm,
                  swa_k_norm=swa_k_norm, swa_sink=swa_sink, mla_q_a_norm=mla_q_a_norm,
                  mla_kv_a_norm=mla_kv_a_norm, mla_w_uq=mla_w_uq, mla_w_ukv=mla_w_ukv,
                  mla_q_norm=mla_q_norm, mla_k_norm=mla_k_norm, b_gate=b_gate, w_br_ssd=w_br_ssd,
                  w_br_swa=w_br_swa, w_br_mla=w_br_mla, w_out=w_out, peer_w_q=peer_w_q,
                  peer_keys=peer_keys, peer_u=peer_u, peer_v=peer_v)
    xl, xc = x, ctx
    for i in range(DEPTH):
        p = {k: v[i] for k, v in params.items()}
        xl, xc = layer(xl, xc, c, c_ctx, p, rope_swa, rope_mla, i < DEPTH - 1)
    return xl
```

```python
import functools

import jax
import jax.numpy as jnp
from jax import lax
from jax.experimental import pallas as pl
from jax.experimental.pallas import tpu as pltpu

D_MODEL = 1024
DEPTH = 2
GRID_W = 64
EPS = 1e-6
NEG_INF = -1e30
ROPE_BASE = 10000.0

SSD_D_INNER = 1024
SSD_HEAD_DIM = 64
SSD_HEADS = 16
SSD_GROUPS = 2
SSD_HEADS_PER_GROUP = 8
SSD_STATE = 64
SSD_CONV = 5
SSD_CHUNK = 128
SSD_CONV_CH = 1280

SWA_HEADS = 8
SWA_KV_HEADS = 2
SWA_HEAD_DIM = 64
SWA_WINDOW = 128
SWA_BLOCK = 128
SWA_SCALE = SWA_HEAD_DIM ** -0.5

MLA_HEADS = 8
MLA_Q_RANK = 384
MLA_KV_RANK = 256
MLA_NOPE = 64
MLA_ROPE = 32
MLA_V = 64
MLA_QK = MLA_NOPE + MLA_ROPE
MLA_BLOCK = 128
MLA_SCALE = MLA_QK ** -0.5

PEER_HEADS = 8
PEER_N_KEYS = 128
PEER_D_KEY = 256
PEER_HALF = 128
PEER_TOPK = 16
PEER_BLOCK = 64

IN_SIZES = (SSD_D_INNER, SSD_CONV_CH, SSD_HEADS, SSD_HEADS,
            SWA_HEADS * SWA_HEAD_DIM, SWA_KV_HEADS * SWA_HEAD_DIM, SWA_KV_HEADS * SWA_HEAD_DIM,
            MLA_Q_RANK, MLA_KV_RANK, MLA_ROPE,
            D_MODEL, D_MODEL, D_MODEL)
IN_WIDTH = sum(IN_SIZES)

LANES = 128


def _round_up(n, m):
    return (n + m - 1) // m * m


def _mm_kernel(a_ref, b_ref, o_ref):
    o_ref[...] = jnp.dot(a_ref[...].astype(jnp.bfloat16), b_ref[...],
                         preferred_element_type=jnp.float32)


def _pick_tile(n, cap, unit):
    t = min(n, cap)
    while n % t or t % unit:
        t -= unit
    return t


def pmatmul(a, w):
    m, k = a.shape
    n = w.shape[1]
    n_pad = _round_up(n, LANES)
    wb = w.astype(jnp.bfloat16)
    if n_pad != n:
        wb = jnp.pad(wb, ((0, 0), (0, n_pad - n)))
    tm = _pick_tile(m, 512, 8)
    tn = _pick_tile(n_pad, 1024, LANES)
    out = pl.pallas_call(
        _mm_kernel,
        grid=(m // tm, n_pad // tn),
        in_specs=[pl.BlockSpec((tm, k), lambda i, j: (i, 0)),
                  pl.BlockSpec((k, tn), lambda i, j: (0, j))],
        out_specs=pl.BlockSpec((tm, tn), lambda i, j: (i, j)),
        out_shape=jax.ShapeDtypeStruct((m, n_pad), jnp.float32),
        compiler_params=pltpu.CompilerParams(dimension_semantics=("parallel", "parallel")),
        name="matmul",
    )(a, wb)
    return out[:, :n] if n_pad != n else out


def mm(a, w):
    lead = a.shape[:-1]
    return pmatmul(a.reshape(-1, a.shape[-1]), w).reshape(lead + (w.shape[1],))


def split_points():
    pts, acc = [], 0
    for s in IN_SIZES[:-1]:
        acc += s
        pts.append(acc)
    return pts


def rms_norm(x, g):
    xf = x.astype(jnp.float32)
    y = xf * lax.rsqrt(jnp.mean(xf * xf, axis=-1, keepdims=True) + EPS)
    return (y * g.astype(jnp.float32)).astype(x.dtype)


def axial_rope(rows, dim):
    pairs = dim // 4
    freqs = ROPE_BASE ** (-jnp.arange(pairs, dtype=jnp.float32) / pairs)
    pos_r = jnp.repeat(jnp.arange(rows, dtype=jnp.float32), GRID_W)
    pos_c = jnp.tile(jnp.arange(GRID_W, dtype=jnp.float32), rows)
    ang = jnp.concatenate([pos_r[:, None] * freqs, pos_c[:, None] * freqs], axis=-1)
    return jnp.cos(ang), jnp.sin(ang)


def apply_rope(x, cos, sin):
    d2 = x.shape[-1] // 2
    x1, x2 = x[..., :d2], x[..., d2:]
    cs, sn = cos[:, None, :], sin[:, None, :]
    return jnp.concatenate([x1 * cs - x2 * sn, x2 * cs + x1 * sn], axis=-1).astype(x.dtype)


def dwconv_centred(u, w, b):
    pad = (SSD_CONV - 1) // 2
    y = lax.conv_general_dilated(u, w[:, None, :].astype(u.dtype), window_strides=(1,),
                                 padding=[(pad, pad)], dimension_numbers=('NWC', 'WIO', 'NWC'),
                                 feature_group_count=u.shape[-1])
    return y + b


def ssd_scan(xh, dt, bm, cm, a, d_skip, init_state, with_y):
    f32 = jnp.float32
    bsz, T, G, Hg, P = xh.shape
    N = bm.shape[-1]
    Q = SSD_CHUNK
    nc = T // Q
    x_c = xh.reshape(bsz, nc, Q, G, Hg, P)
    dt_c = dt.astype(f32).reshape(bsz, nc, Q, G, Hg)
    b_c = bm.reshape(bsz, nc, Q, G, N)
    c_c = cm.reshape(bsz, nc, Q, G, N)
    cum = jnp.cumsum(dt_c * a, axis=2)
    decay_to_end = jnp.exp(cum[:, :, -1:] - cum)
    states = jnp.einsum('bcjgn,bcjgh,bcjghp->bcghpn', b_c, decay_to_end * dt_c, x_c)
    chunk_decay = jnp.exp(cum[:, :, -1])

    def step(s, inp):
        st, dec = inp
        return s * dec[..., None, None] + st, s

    final, prev = lax.scan(step, init_state.astype(f32),
                           (jnp.moveaxis(states, 1, 0), jnp.moveaxis(chunk_decay, 1, 0)))
    if not with_y:
        return None, final
    prev = jnp.moveaxis(prev, 0, 1)
    cum_t = jnp.moveaxis(cum, 2, -1)
    seg = cum_t[..., :, None] - cum_t[..., None, :]
    mask = jnp.tril(jnp.ones((Q, Q), dtype=bool))
    decay = jnp.where(mask, jnp.exp(jnp.where(mask, seg, 0.0)), 0.0)
    cb = jnp.einsum('bcign,bcjgn->bcgij', c_c, b_c)
    y_diag = jnp.einsum('bcgij,bcghij,bcjgh,bcjghp->bcighp', cb, decay, dt_c, x_c)
    y_off = jnp.einsum('bcign,bcghpn,bcigh->bcighp', c_c, prev, jnp.exp(cum))
    y = y_diag + y_off + d_skip[:, :, None] * x_c
    return y.reshape(bsz, T, G, Hg, P), final


def ssd_direction(ctx_in, lat_in, a, d_skip, reverse, with_ctx_out):
    if reverse:
        ctx_in = [jnp.flip(t, 1) for t in ctx_in]
        lat_in = [jnp.flip(t, 1) for t in lat_in]
    s0 = jnp.zeros((lat_in[0].shape[0], SSD_GROUPS, SSD_HEADS_PER_GROUP, SSD_HEAD_DIM, SSD_STATE), jnp.float32)
    y_c, s_c = ssd_scan(*ctx_in, a, d_skip, s0, with_ctx_out)
    y_l, _ = ssd_scan(*lat_in, a, d_skip, s_c, True)
    if reverse:
        y_l = jnp.flip(y_l, 1)
        if with_ctx_out:
            y_c = jnp.flip(y_c, 1)
    return y_l, y_c


def ssd_mixer(lat, ctx, p, with_ctx_out):
    f32 = jnp.float32

    def prep(z, xbc, dtf, dtb):
        u = jax.nn.silu(dwconv_centred(xbc, p['ssd_conv_w'], p['ssd_conv_b']))
        bsz, T, _ = u.shape
        xs, bm, cm = jnp.split(u, [SSD_D_INNER, SSD_D_INNER + SSD_GROUPS * SSD_STATE], axis=-1)
        xs = xs.reshape(bsz, T, SSD_GROUPS, SSD_HEADS_PER_GROUP, SSD_HEAD_DIM)
        bm = bm.reshape(bsz, T, SSD_GROUPS, SSD_STATE)
        cm = cm.reshape(bsz, T, SSD_GROUPS, SSD_STATE)
        dts = [jax.nn.softplus(d.astype(f32) + p['ssd_dt_bias'][i].astype(f32)).reshape(
            bsz, T, SSD_GROUPS, SSD_HEADS_PER_GROUP) for i, d in enumerate((dtf, dtb))]
        return xs, bm, cm, dts

    xl, bl, cl, dtl = prep(*lat)
    xc, bc, cc, dtc = prep(*ctx)
    ys_l, ys_c = [], []
    for direction in range(2):
        a = -jnp.exp(p['ssd_a_log'][direction].astype(f32)).reshape(SSD_GROUPS, SSD_HEADS_PER_GROUP)
        dsk = p['ssd_d'][direction].astype(f32).reshape(SSD_GROUPS, SSD_HEADS_PER_GROUP)
        y_l, y_c = ssd_direction((xc, dtc[direction], bc, cc), (xl, dtl[direction], bl, cl),
                                 a, dsk, direction == 1, with_ctx_out)
        ys_l.append(y_l)
        ys_c.append(y_c)

    def gated_out(y, z):
        y = y.reshape(z.shape[0], z.shape[1], SSD_D_INNER)
        return rms_norm(y * jax.nn.silu(z.astype(f32)), p['ssd_norm_g']).astype(z.dtype)

    out_l = gated_out(ys_l[0] + ys_l[1], lat[0])
    out_c = gated_out(ys_c[0] + ys_c[1], ctx[0]) if with_ctx_out else None
    return out_l, out_c


def swa_mixer(lat, ctx, p, rope, with_ctx_out):
    f32 = jnp.float32
    grp = SWA_HEADS // SWA_KV_HEADS

    def heads(q, k, v):
        bsz, T, _ = q.shape
        q = rms_norm(q.reshape(bsz, T, SWA_HEADS, SWA_HEAD_DIM), p['swa_q_norm'])
        k = rms_norm(k.reshape(bsz, T, SWA_KV_HEADS, SWA_HEAD_DIM), p['swa_k_norm'])
        return q, k, v.reshape(bsz, T, SWA_KV_HEADS, SWA_HEAD_DIM)

    ql, kl, vl = heads(*lat)
    qc, kc, vc = heads(*ctx)
    cos, sin = rope
    ql = apply_rope(ql, cos, sin) * SWA_SCALE
    kl = apply_rope(kl, cos, sin)
    bsz, L = ql.shape[:2]
    nb = L // SWA_BLOCK
    sink = p['swa_sink'].astype(f32).reshape(SWA_KV_HEADS, grp)
    qb = ql.reshape(bsz, nb, SWA_BLOCK, SWA_KV_HEADS, grp, SWA_HEAD_DIM)

    def band(t):
        t = t.reshape(bsz, nb, SWA_BLOCK, SWA_KV_HEADS, SWA_HEAD_DIM)
        tp = jnp.pad(t, ((0, 0), (1, 1), (0, 0), (0, 0), (0, 0)))
        return jnp.concatenate([tp[:, :-2], tp[:, 1:-1], tp[:, 2:]], axis=2)

    kb, vb = band(kl), band(vl)
    s_band = jnp.einsum('bnqkgd,bnjkd->bnkgqj', qb, kb).astype(f32)
    s_ctx = jnp.einsum('bnqkgd,bjkd->bnkgqj', qb, kc).astype(f32)
    qi = jnp.arange(SWA_BLOCK)[None, :, None]
    kj = jnp.arange(3 * SWA_BLOCK)[None, None, :]
    kpos = jnp.arange(nb)[:, None, None] * SWA_BLOCK + kj - SWA_BLOCK
    valid = (jnp.abs(kj - SWA_BLOCK - qi) <= SWA_WINDOW) & (kpos >= 0) & (kpos < L)
    s_band = jnp.where(valid[None, :, None, None], s_band, NEG_INF)
    sink_col = jnp.broadcast_to(sink[None, None, :, :, None, None], s_band.shape[:-1] + (1,))
    prob = jax.nn.softmax(jnp.concatenate([s_band, s_ctx, sink_col], axis=-1), axis=-1).astype(vl.dtype)
    nband = 3 * SWA_BLOCK
    o = (jnp.einsum('bnkgqj,bnjkd->bnqkgd', prob[..., :nband], vb)
         + jnp.einsum('bnkgqj,bjkd->bnqkgd', prob[..., nband:-1], vc))
    y_l = o.reshape(bsz, L, SWA_HEADS * SWA_HEAD_DIM)

    y_c = None
    if with_ctx_out:
        qcb = (qc * SWA_SCALE).reshape(bsz, -1, SWA_KV_HEADS, grp, SWA_HEAD_DIM)
        s = jnp.einsum('bqkgd,bjkd->bkgqj', qcb, kc).astype(f32)
        sink_c = jnp.broadcast_to(sink[None, :, :, None, None], s.shape[:-1] + (1,))
        pc = jax.nn.softmax(jnp.concatenate([s, sink_c], axis=-1), axis=-1)[..., :-1].astype(vc.dtype)
        y_c = jnp.einsum('bkgqj,bjkd->bqkgd', pc, vc).reshape(bsz, -1, SWA_HEADS * SWA_HEAD_DIM)
    return y_l, y_c


def mla_mixer(lat, ctx, p, rope, with_ctx_out):
    f32 = jnp.float32

    def heads(cq, ckv, kr):
        bsz, T, _ = cq.shape
        q = mm(rms_norm(cq, p['mla_q_a_norm']), p['mla_w_uq']).reshape(bsz, T, MLA_HEADS, MLA_QK)
        kv = mm(rms_norm(ckv, p['mla_kv_a_norm']), p['mla_w_ukv']).reshape(bsz, T, MLA_HEADS, MLA_NOPE + MLA_V)
        k = jnp.concatenate([kv[..., :MLA_NOPE],
                             jnp.broadcast_to(kr[:, :, None, :], (bsz, T, MLA_HEADS, MLA_ROPE))], axis=-1)
        return rms_norm(q, p['mla_q_norm']), rms_norm(k, p['mla_k_norm']), kv[..., MLA_NOPE:]

    cos, sin = rope

    def rope_tail(t):
        return jnp.concatenate([t[..., :MLA_NOPE], apply_rope(t[..., MLA_NOPE:], cos, sin)], axis=-1)

    ql, kl, vl = heads(*lat)
    qc, kc, vc = heads(*ctx)
    ql = rope_tail(ql) * MLA_SCALE
    kl = rope_tail(kl)
    bsz, L = ql.shape[:2]
    nb = L // MLA_BLOCK
    k_all = jnp.concatenate([kc, kl], axis=1)
    v_all = jnp.concatenate([vc, vl], axis=1)

    def attend(qb, k, v):
        s = jnp.einsum('bqhd,bkhd->bhqk', qb, k).astype(f32)
        pr = jax.nn.softmax(s, axis=-1).astype(v.dtype)
        return jnp.einsum('bhqk,bkhd->bqhd', pr, v)

    q_blocks = jnp.swapaxes(ql.reshape(bsz, nb, MLA_BLOCK, MLA_HEADS, MLA_QK), 0, 1)
    o = lax.map(lambda qb: attend(qb, k_all, v_all), q_blocks)
    y_l = jnp.swapaxes(o, 0, 1).reshape(bsz, L, MLA_HEADS * MLA_V)
    y_c = None
    if with_ctx_out:
        y_c = attend(qc * MLA_SCALE, kc, vc).reshape(bsz, -1, MLA_HEADS * MLA_V)
    return y_l, y_c


def token_mixers(h_l, h_c, p, rope_swa, rope_mla, with_ctx_out):
    pts = split_points()
    pl_ = jnp.split(mm(h_l, p['w_in']), pts, axis=-1)
    pc = jnp.split(mm(h_c, p['w_in']), pts, axis=-1)
    y_ssd_l, y_ssd_c = ssd_mixer(pl_[0:4], pc[0:4], p, with_ctx_out)
    y_swa_l, y_swa_c = swa_mixer(pl_[4:7], pc[4:7], p, rope_swa, with_ctx_out)
    y_mla_l, y_mla_c = mla_mixer(pl_[7:10], pc[7:10], p, rope_mla, with_ctx_out)
    b_ga, b_gb, b_gc = jnp.split(p['b_gate'], 3)

    def merge(ys, yw, ym, gates):
        ga = jax.nn.sigmoid(gates[0] + b_ga)
        gb = jax.nn.sigmoid(gates[1] + b_gb)
        gc = jax.nn.sigmoid(gates[2] + b_gc)
        m = ga * mm(ys, p['w_br_ssd']) + gb * mm(yw, p['w_br_swa']) + gc * mm(ym, p['w_br_mla'])
        return mm(m, p['w_out'])

    out_l = merge(y_ssd_l, y_swa_l, y_mla_l, pl_[10:13])
    out_c = merge(y_ssd_c, y_swa_c, y_mla_c, pc[10:13]) if with_ctx_out else None
    return out_l, out_c


PEER_ROUTE_TM = 256
PEER_TM = 512
PEER_EC = 512
PEER_VMEM_LIMIT = 56 * 1024 * 1024
_SQRT_HALF = 0.7071067811865476


def _top16_rows(s):
    rows = []
    for _ in range(PEER_TOPK):
        m = jnp.max(s, axis=0, keepdims=True)
        rows.append(m)
        s = jnp.where(s == m, -jnp.inf, s)
    return jnp.concatenate(rows, axis=0)


def _peer_route_kernel(h_ref, wq_ref, keys_ref, s0_ref, s1_ref, a_ref, b_ref, tau_ref, q_scr):
    f32 = jnp.float32
    q = jnp.dot(h_ref[...].astype(jnp.bfloat16), wq_ref[...], preferred_element_type=f32)
    q_scr[...] = q.astype(jnp.bfloat16)

    def head_body(hd, carry):
        def scores(p):
            col = pl.multiple_of((hd * 2 + p) * PEER_HALF, PEER_HALF)
            qs = q_scr[:, pl.ds(col, PEER_HALF)]
            return lax.dot_general(keys_ref[p * PEER_HEADS + hd], qs, (((1,), (1,)), ((), ())),
                                   preferred_element_type=f32)

        s0 = scores(0)
        s1 = scores(1)
        a_top = _top16_rows(s0)
        b_top = _top16_rows(s1)
        cand = [a_top[0:1] + b_top]
        cand += [a_top[r:r + 1] + b_top[0:8] for r in range(1, 8)]
        cand.append(a_top[8:16] + b_top[0:1])
        best = _top16_rows(jnp.concatenate(cand, axis=0))
        z = jnp.sum(jnp.exp(best - best[0:1]), axis=0, keepdims=True)
        s0_ref[hd] = s0
        s1_ref[hd] = s1
        a_ref[hd] = jnp.exp(s0 - a_top[0:1])
        b_ref[hd] = jnp.exp(s1 - b_top[0:1]) / z
        tau_ref[hd] = best[PEER_TOPK - 1:PEER_TOPK]
        return carry

    lax.fori_loop(0, PEER_HEADS, head_body, 0)


def _gelu_exact(x):
    return 0.5 * x * (1.0 + lax.erf(x * _SQRT_HALF))


def _peer_expert_kernel(h_ref, u_ref, vt_ref, s0_ref, s1_ref, a_ref, b_ref, tau_ref, o_ref, acc_ref, g_scr):
    f32 = jnp.float32
    j = pl.program_id(1)
    rows_per_step = PEER_EC // PEER_N_KEYS

    @pl.when(j == 0)
    def _():
        acc_ref[...] = jnp.zeros_like(acc_ref)

    act = lax.dot_general(u_ref[...], h_ref[...], (((1,), (1,)), ((), ())),
                          preferred_element_type=f32)
    for ii in range(rows_per_step):
        i = j * rows_per_step + ii
        w = jnp.zeros((PEER_N_KEYS, PEER_TM), f32)
        for hd in range(PEER_HEADS):
            s0_row = s0_ref[hd, pl.ds(i, 1), :]
            a_row = a_ref[hd, pl.ds(i, 1), :]
            keep = (s1_ref[hd] + s0_row) >= tau_ref[hd]
            w = w + jnp.where(keep, b_ref[hd], 0.0) * a_row
        blk = slice(ii * PEER_N_KEYS, (ii + 1) * PEER_N_KEYS)
        g_scr[blk, :] = (w * _gelu_exact(act[blk, :])).astype(jnp.bfloat16)
    acc_ref[...] += jnp.dot(vt_ref[...], g_scr[...], preferred_element_type=f32)

    @pl.when(j == pl.num_programs(1) - 1)
    def _():
        o_ref[...] = acc_ref[...].T


def peer_ffn(h, w_q, sub_keys, u_tab, v_tab):
    bsz, T, D = h.shape
    n_tok = bsz * T
    n_exp = u_tab.shape[0]
    hf = h.reshape(n_tok, D)
    f32 = jnp.float32
    bf16 = jnp.bfloat16
    keys = sub_keys.reshape(2 * PEER_HEADS, PEER_N_KEYS, PEER_HALF).astype(bf16)
    tm = PEER_ROUTE_TM
    big = jax.ShapeDtypeStruct((PEER_HEADS, PEER_N_KEYS, n_tok), f32)
    big_spec = pl.BlockSpec((PEER_HEADS, PEER_N_KEYS, tm), lambda t: (0, 0, t))
    s0, s1, a, b, tau = pl.pallas_call(
        _peer_route_kernel,
        grid=(n_tok // tm,),
        in_specs=[pl.BlockSpec((tm, D), lambda t: (t, 0)),
                  pl.BlockSpec((D, PEER_HEADS * PEER_D_KEY), lambda t: (0, 0)),
                  pl.BlockSpec((2 * PEER_HEADS, PEER_N_KEYS, PEER_HALF), lambda t: (0, 0, 0))],
        out_specs=[big_spec, big_spec, big_spec, big_spec,
                   pl.BlockSpec((PEER_HEADS, 1, tm), lambda t: (0, 0, t))],
        out_shape=[big, big, big, big, jax.ShapeDtypeStruct((PEER_HEADS, 1, n_tok), f32)],
        scratch_shapes=[pltpu.VMEM((tm, PEER_HEADS * PEER_D_KEY), bf16)],
        compiler_params=pltpu.CompilerParams(dimension_semantics=("parallel",),
                                             vmem_limit_bytes=PEER_VMEM_LIMIT),
        name="peer_route",
    )(hf, w_q.astype(bf16), keys)

    tm = PEER_TM
    ec = PEER_EC
    big_spec = pl.BlockSpec((PEER_HEADS, PEER_N_KEYS, tm), lambda t, j: (0, 0, t))
    out = pl.pallas_call(
        _peer_expert_kernel,
        grid=(n_tok // tm, n_exp // ec),
        in_specs=[pl.BlockSpec((tm, D), lambda t, j: (t, 0)),
                  pl.BlockSpec((ec, D), lambda t, j: (j, 0)),
                  pl.BlockSpec((D, ec), lambda t, j: (0, j)),
                  big_spec, big_spec, big_spec, big_spec,
                  pl.BlockSpec((PEER_HEADS, 1, tm), lambda t, j: (0, 0, t))],
        out_specs=pl.BlockSpec((tm, D), lambda t, j: (t, 0)),
        out_shape=jax.ShapeDtypeStruct((n_tok, D), f32),
        scratch_shapes=[pltpu.VMEM((D, tm), f32), pltpu.VMEM((ec, tm), bf16)],
        compiler_params=pltpu.CompilerParams(dimension_semantics=("parallel", "arbitrary"),
                                             vmem_limit_bytes=PEER_VMEM_LIMIT),
        name="peer_expert",
    )(hf.astype(bf16), u_tab.astype(bf16), v_tab.T.astype(bf16), s0, s1, a, b, tau)
    return out.reshape(bsz, T, D)


def layer(xl, xc, c, c_ctx, p, rope_swa, rope_mla, with_ctx_out):
    mod_l = (jax.nn.silu(c) @ p['w_mod'] + p['b_mod'])[:, None, :]
    mod_c = (jax.nn.silu(c_ctx) @ p['w_mod'] + p['b_mod'])[None, None, :]
    sh1_l, sc1_l, g1_l, sh2_l, sc2_l, g2_l = jnp.split(mod_l, 6, axis=-1)
    sh1_c, sc1_c, g1_c, sh2_c, sc2_c, g2_c = jnp.split(mod_c, 6, axis=-1)
    h_l = rms_norm(xl, p['norm1_g']) * (1 + sc1_l) + sh1_l
    h_c = rms_norm(xc, p['norm1_g']) * (1 + sc1_c) + sh1_c
    mix_l, mix_c = token_mixers(h_l, h_c, p, rope_swa, rope_mla, with_ctx_out)
    xl = xl + g1_l * mix_l
    h2_l = rms_norm(xl, p['norm2_g']) * (1 + sc2_l) + sh2_l
    if with_ctx_out:
        xc = xc + g1_c * mix_c
        h2_c = rms_norm(xc, p['norm2_g']) * (1 + sc2_c) + sh2_c
        n_ctx = xc.shape[1]
        f = peer_ffn(jnp.concatenate([h2_c, h2_l], axis=1), p['peer_w_q'], p['peer_keys'], p['peer_u'], p['peer_v'])
        f_c, f_l = f[:, :n_ctx], f[:, n_ctx:]
        xc = xc + g2_c * f_c
    else:
        f_l = peer_ffn(h2_l, p['peer_w_q'], p['peer_keys'], p['peer_u'], p['peer_v'])
    xl = xl + g2_l * f_l
    return xl, xc


def kernel(x, c, ctx, c_ctx, w_mod, b_mod, norm1_g, norm2_g, w_in, ssd_conv_w, ssd_conv_b, ssd_dt_bias,
           ssd_a_log, ssd_d, ssd_norm_g, swa_q_norm, swa_k_norm, swa_sink, mla_q_a_norm, mla_kv_a_norm,
           mla_w_uq, mla_w_ukv, mla_q_norm, mla_k_norm, b_gate, w_br_ssd, w_br_swa, w_br_mla, w_out,
           peer_w_q, peer_keys, peer_u, peer_v):
    L = x.shape[1]
    rows = L // GRID_W
    rope_swa = axial_rope(rows, SWA_HEAD_DIM)
    rope_mla = axial_rope(rows, MLA_ROPE)
    params = dict(w_mod=w_mod, b_mod=b_mod, norm1_g=norm1_g, norm2_g=norm2_g, w_in=w_in,
                  ssd_conv_w=ssd_conv_w, ssd_conv_b=ssd_conv_b, ssd_dt_bias=ssd_dt_bias,
                  ssd_a_log=ssd_a_log, ssd_d=ssd_d, ssd_norm_g=ssd_norm_g, swa_q_norm=swa_q_norm,
                  swa_k_norm=swa_k_norm, swa_sink=swa_sink, mla_q_a_norm=mla_q_a_norm,
                  mla_kv_a_norm=mla_kv_a_norm, mla_w_uq=mla_w_uq, mla_w_ukv=mla_w_ukv,
                  mla_q_norm=mla_q_norm, mla_k_norm=mla_k_norm, b_gate=b_gate, w_br_ssd=w_br_ssd,
                  w_br_swa=w_br_swa, w_br_mla=w_br_mla, w_out=w_out, peer_w_q=peer_w_q,
                  peer_keys=peer_keys, peer_u=peer_u, peer_v=peer_v)
    xl, xc = x, ctx
    for i in range(DEPTH):
        p = {k: v[i] for k, v in params.items()}
        xl, xc = layer(xl, xc, c, c_ctx, p, rope_swa, rope_mla, i < DEPTH - 1)
    return xl
```

```python
import functools

import jax
import jax.numpy as jnp
from jax import lax
from jax.experimental import pallas as pl
from jax.experimental.pallas import tpu as pltpu

D_MODEL = 1024
DEPTH = 2
GRID_W = 64
EPS = 1e-6
NEG_INF = -1e30
ROPE_BASE = 10000.0

SSD_D_INNER = 1024
SSD_HEAD_DIM = 64
SSD_HEADS = 16
SSD_GROUPS = 2
SSD_HEADS_PER_GROUP = 8
SSD_STATE = 64
SSD_CONV = 5
SSD_CHUNK = 128
SSD_CONV_CH = 1280

SWA_HEADS = 8
SWA_KV_HEADS = 2
SWA_HEAD_DIM = 64
SWA_WINDOW = 128
SWA_BLOCK = 128
SWA_SCALE = SWA_HEAD_DIM ** -0.5

MLA_HEADS = 8
MLA_Q_RANK = 384
MLA_KV_RANK = 256
MLA_NOPE = 64
MLA_ROPE = 32
MLA_V = 64
MLA_QK = MLA_NOPE + MLA_ROPE
MLA_BLOCK = 128
MLA_SCALE = MLA_QK ** -0.5
MLA_TQ = 512
MLA_TK = 768

PEER_HEADS = 8
PEER_N_KEYS = 128
PEER_D_KEY = 256
PEER_HALF = 128
PEER_TOPK = 16
PEER_BLOCK = 64

IN_SIZES = (SSD_D_INNER, SSD_CONV_CH, SSD_HEADS, SSD_HEADS,
            SWA_HEADS * SWA_HEAD_DIM, SWA_KV_HEADS * SWA_HEAD_DIM, SWA_KV_HEADS * SWA_HEAD_DIM,
            MLA_Q_RANK, MLA_KV_RANK, MLA_ROPE,
            D_MODEL, D_MODEL, D_MODEL)
IN_WIDTH = sum(IN_SIZES)

LANES = 128


def _round_up(n, m):
    return (n + m - 1) // m * m


def _mm_kernel(a_ref, b_ref, o_ref):
    o_ref[...] = jnp.dot(a_ref[...].astype(jnp.bfloat16), b_ref[...],
                         preferred_element_type=jnp.float32)


def _pick_tile(n, cap, unit):
    t = min(n, cap)
    while n % t or t % unit:
        t -= unit
    return t


def pmatmul(a, w):
    m, k = a.shape
    n = w.shape[1]
    n_pad = _round_up(n, LANES)
    wb = w.astype(jnp.bfloat16)
    if n_pad != n:
        wb = jnp.pad(wb, ((0, 0), (0, n_pad - n)))
    tm = _pick_tile(m, 512, 8)
    tn = _pick_tile(n_pad, 1024, LANES)
    out = pl.pallas_call(
        _mm_kernel,
        grid=(m // tm, n_pad // tn),
        in_specs=[pl.BlockSpec((tm, k), lambda i, j: (i, 0)),
                  pl.BlockSpec((k, tn), lambda i, j: (0, j))],
        out_specs=pl.BlockSpec((tm, tn), lambda i, j: (i, j)),
        out_shape=jax.ShapeDtypeStruct((m, n_pad), jnp.float32),
        compiler_params=pltpu.CompilerParams(dimension_semantics=("parallel", "parallel")),
        name="matmul",
    )(a, wb)
    return out[:, :n] if n_pad != n else out


def mm(a, w):
    lead = a.shape[:-1]
    return pmatmul(a.reshape(-1, a.shape[-1]), w).reshape(lead + (w.shape[1],))


def _flash_kernel(q_ref, k_ref, v_ref, o_ref, *, tk):
    f32 = jnp.float32
    q = q_ref[0, 0]
    tq = q.shape[0]
    dv = v_ref.shape[-1]
    n_chunks = k_ref.shape[2] // tk

    def body(c, carry):
        m, l, acc = carry
        off = pl.multiple_of(c * tk, tk)
        k = k_ref[0, 0, pl.ds(off, tk), :]
        v = v_ref[0, 0, pl.ds(off, tk), :]
        s = lax.dot_general(q, k, (((1,), (1,)), ((), ())), preferred_element_type=f32)
        m_new = jnp.maximum(m, jnp.max(s, axis=-1, keepdims=True))
        p = jnp.exp(s - m_new)
        alpha = jnp.exp(m - m_new)
        l = alpha * l + jnp.sum(p, axis=-1, keepdims=True)
        acc = alpha * acc + jnp.dot(p.astype(jnp.bfloat16), v, preferred_element_type=f32)
        return m_new, l, acc

    init = (jnp.full((tq, 1), -jnp.inf, f32), jnp.zeros((tq, 1), f32), jnp.zeros((tq, dv), f32))
    _, l, acc = lax.fori_loop(0, n_chunks, body, init)
    o_ref[0, 0] = acc / l


def flash_attention(q, k, v, tq, tk):
    bsz, nh, lq, dq = q.shape
    nk, dv = v.shape[2], v.shape[3]
    bf16 = jnp.bfloat16
    return pl.pallas_call(
        functools.partial(_flash_kernel, tk=tk),
        grid=(bsz, nh, lq // tq),
        in_specs=[pl.BlockSpec((1, 1, tq, dq), lambda b, h, i: (b, h, i, 0)),
                  pl.BlockSpec((1, 1, nk, dq), lambda b, h, i: (b, h, 0, 0)),
                  pl.BlockSpec((1, 1, nk, dv), lambda b, h, i: (b, h, 0, 0))],
        out_specs=pl.BlockSpec((1, 1, tq, dv), lambda b, h, i: (b, h, i, 0)),
        out_shape=jax.ShapeDtypeStruct((bsz, nh, lq, dv), jnp.float32),
        compiler_params=pltpu.CompilerParams(dimension_semantics=("parallel", "parallel", "parallel"),
                                             vmem_limit_bytes=PEER_VMEM_LIMIT),
        name="flash_attention",
    )(q.astype(bf16), k.astype(bf16), v.astype(bf16))


SSD_PAIR = 2 * SSD_HEAD_DIM


def _softplus(x):
    return jnp.maximum(x, 0.0) + jnp.log1p(jnp.exp(-jnp.abs(x)))


def _ssd_kernel(u_ref, dt_ref, dtt_ref, bias_r_ref, bias_c_ref, alog_r_ref, alog_c_ref, dx_ref,
                y_ref, state_ref, *, reverse):
    f32 = jnp.float32
    bf16 = jnp.bfloat16
    q = SSD_CHUNK
    hi = lax.Precision.HIGHEST

    @pl.when(pl.program_id(1) == 0)
    def _():
        state_ref[...] = jnp.zeros_like(state_ref)

    row = lax.broadcasted_iota(jnp.int32, (q, q), 0)
    col = lax.broadcasted_iota(jnp.int32, (q, q), 1)
    before = (col >= row) if reverse else (col <= row)
    tri = before.astype(f32)

    dt = _softplus(dt_ref[0] + bias_r_ref[...])
    dtt = _softplus(dtt_ref[0] + bias_c_ref[...])
    a_r = -jnp.exp(alog_r_ref[...])
    a_c = -jnp.exp(alog_c_ref[...])
    cum = jnp.dot(tri, dt * a_r, precision=hi, preferred_element_type=f32)
    cumt = lax.dot_general(dtt * a_c, tri, (((1,), (1,)), ((), ())), precision=hi,
                           preferred_element_type=f32)
    tot = cum[0:1, :] if reverse else cum[q - 1:q, :]
    ecum = jnp.exp(cum)
    wgt = jnp.exp(tot - cum) * dt
    etot = jnp.exp(tot)

    lane_lo = lax.broadcasted_iota(jnp.int32, (q, SSD_PAIR), 1) < SSD_HEAD_DIM
    lane_lo_row = lane_lo[0:1, :]
    heads_per_group = SSD_HEADS_PER_GROUP
    for g in range(SSD_GROUPS):
        b_off = SSD_D_INNER + g * SSD_STATE
        c_off = SSD_D_INNER + SSD_GROUPS * SSD_STATE + g * SSD_STATE
        bg = u_ref[0, :, b_off:b_off + SSD_STATE].astype(bf16)
        cg = u_ref[0, :, c_off:c_off + SSD_STATE].astype(bf16)
        cb = lax.dot_general(cg, bg, (((1,), (1,)), ((), ())), preferred_element_type=f32)
        for kk in range(heads_per_group // 2):
            k = g * (heads_per_group // 2) + kk
            lanes = slice(k * SSD_PAIR, (k + 1) * SSD_PAIR)
            xp = u_ref[0, :, lanes]
            y = dx_ref[:, lanes] * xp
            for half in range(2):
                hh = 2 * k + half
                seg = cum[:, hh:hh + 1] - cumt[hh:hh + 1, :]
                decay = jnp.where(before, jnp.exp(seg), 0.0) * dtt[hh:hh + 1, :]
                m = (cb * decay).astype(bf16)
                xm = jnp.where(lane_lo if half == 0 else jnp.logical_not(lane_lo), xp, 0.0).astype(bf16)
                y = y + jnp.dot(m, xm, preferred_element_type=f32)
            e_pair = jnp.where(lane_lo, ecum[:, 2 * k:2 * k + 1], ecum[:, 2 * k + 1:2 * k + 2])
            w_pair = jnp.where(lane_lo, wgt[:, 2 * k:2 * k + 1], wgt[:, 2 * k + 1:2 * k + 2])
            t_pair = jnp.where(lane_lo_row, etot[:, 2 * k:2 * k + 1], etot[:, 2 * k + 1:2 * k + 2])
            prev = state_ref[:, lanes]
            y = y + jnp.dot(cg, prev.astype(bf16), preferred_element_type=f32) * e_pair
            y_ref[0, :, lanes] = y
            xw = (xp * w_pair).astype(bf16)
            s_new = lax.dot_general(bg, xw, (((0,), (0,)), ((), ())), preferred_element_type=f32)
            state_ref[:, lanes] = prev * t_pair + s_new


def ssd_scan_dir(u, dt_raw, bias, a_log, d_skip, reverse, n_ctx):
    bsz, t_all, _ = u.shape
    q = SSD_CHUNK
    nc = t_all // q
    ncc = n_ctx // q
    if reverse:
        def chunk(c):
            return jnp.where(c < ncc, ncc - 1 - c, nc + ncc - 1 - c)
    else:
        def chunk(c):
            return c
    f32 = jnp.float32
    row = lambda v: v.astype(f32).reshape(1, SSD_HEADS)
    colv = lambda v: v.astype(f32).reshape(SSD_HEADS, 1)
    small_r = pl.BlockSpec((1, SSD_HEADS), lambda b, c: (0, 0))
    small_c = pl.BlockSpec((SSD_HEADS, 1), lambda b, c: (0, 0))
    return pl.pallas_call(
        functools.partial(_ssd_kernel, reverse=reverse),
        grid=(bsz, nc),
        in_specs=[pl.BlockSpec((1, q, SSD_CONV_CH), lambda b, c: (b, chunk(c), 0)),
                  pl.BlockSpec((1, q, SSD_HEADS), lambda b, c: (b, chunk(c), 0)),
                  pl.BlockSpec((1, SSD_HEADS, q), lambda b, c: (b, 0, chunk(c))),
                  small_r, small_c, small_r, small_c,
                  pl.BlockSpec((1, SSD_D_INNER), lambda b, c: (0, 0))],
        out_specs=pl.BlockSpec((1, q, SSD_D_INNER), lambda b, c: (b, chunk(c), 0)),
        out_shape=jax.ShapeDtypeStruct((bsz, t_all, SSD_D_INNER), f32),
        scratch_shapes=[pltpu.VMEM((SSD_STATE, SSD_D_INNER), f32)],
        compiler_params=pltpu.CompilerParams(dimension_semantics=("parallel", "arbitrary")),
        name="ssd_scan_bwd" if reverse else "ssd_scan_fwd",
    )(u, dt_raw, jnp.swapaxes(dt_raw, 1, 2), row(bias), colv(bias), row(a_log), colv(a_log),
      jnp.repeat(d_skip.astype(f32), SSD_HEAD_DIM).reshape(1, SSD_D_INNER))


def split_points():
    pts, acc = [], 0
    for s in IN_SIZES[:-1]:
        acc += s
        pts.append(acc)
    return pts


def rms_norm(x, g):
    xf = x.astype(jnp.float32)
    y = xf * lax.rsqrt(jnp.mean(xf * xf, axis=-1, keepdims=True) + EPS)
    return (y * g.astype(jnp.float32)).astype(x.dtype)


def axial_rope(rows, dim):
    pairs = dim // 4
    freqs = ROPE_BASE ** (-jnp.arange(pairs, dtype=jnp.float32) / pairs)
    pos_r = jnp.repeat(jnp.arange(rows, dtype=jnp.float32), GRID_W)
    pos_c = jnp.tile(jnp.arange(GRID_W, dtype=jnp.float32), rows)
    ang = jnp.concatenate([pos_r[:, None] * freqs, pos_c[:, None] * freqs], axis=-1)
    return jnp.cos(ang), jnp.sin(ang)


def apply_rope(x, cos, sin):
    d2 = x.shape[-1] // 2
    x1, x2 = x[..., :d2], x[..., d2:]
    cs, sn = cos[:, None, :], sin[:, None, :]
    return jnp.concatenate([x1 * cs - x2 * sn, x2 * cs + x1 * sn], axis=-1).astype(x.dtype)


def dwconv_centred(u, w, b):
    pad = (SSD_CONV - 1) // 2
    y = lax.conv_general_dilated(u, w[:, None, :].astype(u.dtype), window_strides=(1,),
                                 padding=[(pad, pad)], dimension_numbers=('NWC', 'WIO', 'NWC'),
                                 feature_group_count=u.shape[-1])
    return y + b


def ssd_scan(xh, dt, bm, cm, a, d_skip, init_state, with_y):
    f32 = jnp.float32
    bsz, T, G, Hg, P = xh.shape
    N = bm.shape[-1]
    Q = SSD_CHUNK
    nc = T // Q
    x_c = xh.reshape(bsz, nc, Q, G, Hg, P)
    dt_c = dt.astype(f32).reshape(bsz, nc, Q, G, Hg)
    b_c = bm.reshape(bsz, nc, Q, G, N)
    c_c = cm.reshape(bsz, nc, Q, G, N)
    cum = jnp.cumsum(dt_c * a, axis=2)
    decay_to_end = jnp.exp(cum[:, :, -1:] - cum)
    states = jnp.einsum('bcjgn,bcjgh,bcjghp->bcghpn', b_c, decay_to_end * dt_c, x_c)
    chunk_decay = jnp.exp(cum[:, :, -1])

    def step(s, inp):
        st, dec = inp
        return s * dec[..., None, None] + st, s

    final, prev = lax.scan(step, init_state.astype(f32),
                           (jnp.moveaxis(states, 1, 0), jnp.moveaxis(chunk_decay, 1, 0)))
    if not with_y:
        return None, final
    prev = jnp.moveaxis(prev, 0, 1)
    cum_t = jnp.moveaxis(cum, 2, -1)
    seg = cum_t[..., :, None] - cum_t[..., None, :]
    mask = jnp.tril(jnp.ones((Q, Q), dtype=bool))
    decay = jnp.where(mask, jnp.exp(jnp.where(mask, seg, 0.0)), 0.0)
    cb = jnp.einsum('bcign,bcjgn->bcgij', c_c, b_c)
    y_diag = jnp.einsum('bcgij,bcghij,bcjgh,bcjghp->bcighp', cb, decay, dt_c, x_c)
    y_off = jnp.einsum('bcign,bcghpn,bcigh->bcighp', c_c, prev, jnp.exp(cum))
    y = y_diag + y_off + d_skip[:, :, None] * x_c
    return y.reshape(bsz, T, G, Hg, P), final


def ssd_direction(ctx_in, lat_in, a, d_skip, reverse, with_ctx_out):
    if reverse:
        ctx_in = [jnp.flip(t, 1) for t in ctx_in]
        lat_in = [jnp.flip(t, 1) for t in lat_in]
    s0 = jnp.zeros((lat_in[0].shape[0], SSD_GROUPS, SSD_HEADS_PER_GROUP, SSD_HEAD_DIM, SSD_STATE), jnp.float32)
    y_c, s_c = ssd_scan(*ctx_in, a, d_skip, s0, with_ctx_out)
    y_l, _ = ssd_scan(*lat_in, a, d_skip, s_c, True)
    if reverse:
        y_l = jnp.flip(y_l, 1)
        if with_ctx_out:
            y_c = jnp.flip(y_c, 1)
    return y_l, y_c


def ssd_mixer(lat, ctx, p, with_ctx_out):
    f32 = jnp.float32

    def conv_act(xbc):
        return jax.nn.silu(dwconv_centred(xbc, p['ssd_conv_w'], p['ssd_conv_b']))

    n_ctx = ctx[1].shape[1]
    u_all = jnp.concatenate([conv_act(ctx[1]), conv_act(lat[1])], axis=1)
    y_all = None
    for direction in range(2):
        dt_raw = jnp.concatenate([ctx[2 + direction], lat[2 + direction]], axis=1)
        y_d = ssd_scan_dir(u_all, dt_raw, p['ssd_dt_bias'][direction], p['ssd_a_log'][direction],
                           p['ssd_d'][direction], direction == 1, n_ctx)
        y_all = y_d if y_all is None else y_all + y_d

    def gated_out(y, z):
        return rms_norm(y * jax.nn.silu(z.astype(f32)), p['ssd_norm_g']).astype(z.dtype)

    out_l = gated_out(y_all[:, n_ctx:], lat[0])
    out_c = gated_out(y_all[:, :n_ctx], ctx[0]) if with_ctx_out else None
    return out_l, out_c


def swa_mixer(lat, ctx, p, rope, with_ctx_out):
    f32 = jnp.float32
    grp = SWA_HEADS // SWA_KV_HEADS

    def heads(q, k, v):
        bsz, T, _ = q.shape
        q = rms_norm(q.reshape(bsz, T, SWA_HEADS, SWA_HEAD_DIM), p['swa_q_norm'])
        k = rms_norm(k.reshape(bsz, T, SWA_KV_HEADS, SWA_HEAD_DIM), p['swa_k_norm'])
        return q, k, v.reshape(bsz, T, SWA_KV_HEADS, SWA_HEAD_DIM)

    ql, kl, vl = heads(*lat)
    qc, kc, vc = heads(*ctx)
    cos, sin = rope
    ql = apply_rope(ql, cos, sin) * SWA_SCALE
    kl = apply_rope(kl, cos, sin)
    bsz, L = ql.shape[:2]
    nb = L // SWA_BLOCK
    sink = p['swa_sink'].astype(f32).reshape(SWA_KV_HEADS, grp)
    qb = ql.reshape(bsz, nb, SWA_BLOCK, SWA_KV_HEADS, grp, SWA_HEAD_DIM)

    def band(t):
        t = t.reshape(bsz, nb, SWA_BLOCK, SWA_KV_HEADS, SWA_HEAD_DIM)
        tp = jnp.pad(t, ((0, 0), (1, 1), (0, 0), (0, 0), (0, 0)))
        return jnp.concatenate([tp[:, :-2], tp[:, 1:-1], tp[:, 2:]], axis=2)

    kb, vb = band(kl), band(vl)
    s_band = jnp.einsum('bnqkgd,bnjkd->bnkgqj', qb, kb).astype(f32)
    s_ctx = jnp.einsum('bnqkgd,bjkd->bnkgqj', qb, kc).astype(f32)
    qi = jnp.arange(SWA_BLOCK)[None, :, None]
    kj = jnp.arange(3 * SWA_BLOCK)[None, None, :]
    kpos = jnp.arange(nb)[:, None, None] * SWA_BLOCK + kj - SWA_BLOCK
    valid = (jnp.abs(kj - SWA_BLOCK - qi) <= SWA_WINDOW) & (kpos >= 0) & (kpos < L)
    s_band = jnp.where(valid[None, :, None, None], s_band, NEG_INF)
    sink_col = jnp.broadcast_to(sink[None, None, :, :, None, None], s_band.shape[:-1] + (1,))
    prob = jax.nn.softmax(jnp.concatenate([s_band, s_ctx, sink_col], axis=-1), axis=-1).astype(vl.dtype)
    nband = 3 * SWA_BLOCK
    o = (jnp.einsum('bnkgqj,bnjkd->bnqkgd', prob[..., :nband], vb)
         + jnp.einsum('bnkgqj,bjkd->bnqkgd', prob[..., nband:-1], vc))
    y_l = o.reshape(bsz, L, SWA_HEADS * SWA_HEAD_DIM)

    y_c = None
    if with_ctx_out:
        qcb = (qc * SWA_SCALE).reshape(bsz, -1, SWA_KV_HEADS, grp, SWA_HEAD_DIM)
        s = jnp.einsum('bqkgd,bjkd->bkgqj', qcb, kc).astype(f32)
        sink_c = jnp.broadcast_to(sink[None, :, :, None, None], s.shape[:-1] + (1,))
        pc = jax.nn.softmax(jnp.concatenate([s, sink_c], axis=-1), axis=-1)[..., :-1].astype(vc.dtype)
        y_c = jnp.einsum('bkgqj,bjkd->bqkgd', pc, vc).reshape(bsz, -1, SWA_HEADS * SWA_HEAD_DIM)
    return y_l, y_c


def mla_mixer(lat, ctx, p, rope, with_ctx_out):
    f32 = jnp.float32

    def heads(cq, ckv, kr):
        bsz, T, _ = cq.shape
        q = mm(rms_norm(cq, p['mla_q_a_norm']), p['mla_w_uq']).reshape(bsz, T, MLA_HEADS, MLA_QK)
        kv = mm(rms_norm(ckv, p['mla_kv_a_norm']), p['mla_w_ukv']).reshape(bsz, T, MLA_HEADS, MLA_NOPE + MLA_V)
        k = jnp.concatenate([kv[..., :MLA_NOPE],
                             jnp.broadcast_to(kr[:, :, None, :], (bsz, T, MLA_HEADS, MLA_ROPE))], axis=-1)
        return rms_norm(q, p['mla_q_norm']), rms_norm(k, p['mla_k_norm']), kv[..., MLA_NOPE:]

    cos, sin = rope

    def rope_tail(t):
        return jnp.concatenate([t[..., :MLA_NOPE], apply_rope(t[..., MLA_NOPE:], cos, sin)], axis=-1)

    ql, kl, vl = heads(*lat)
    qc, kc, vc = heads(*ctx)
    ql = rope_tail(ql) * MLA_SCALE
    kl = rope_tail(kl)
    bsz, L = ql.shape[:2]
    nb = L // MLA_BLOCK
    k_all = jnp.concatenate([kc, kl], axis=1)
    v_all = jnp.concatenate([vc, vl], axis=1)

    def attend(qb, k, v, tq, tk):
        hm = lambda t: jnp.swapaxes(t, 1, 2)
        o = flash_attention(hm(qb), hm(k), hm(v), tq, tk)
        return hm(o).reshape(qb.shape[0], qb.shape[1], MLA_HEADS * MLA_V)

    y_l = attend(ql, k_all, v_all, MLA_TQ, MLA_TK)
    y_c = None
    if with_ctx_out:
        n_ctx = qc.shape[1]
        y_c = attend(qc * MLA_SCALE, kc, vc, n_ctx, n_ctx)
    return y_l, y_c


def token_mixers(h_l, h_c, p, rope_swa, rope_mla, with_ctx_out):
    pts = split_points()
    pl_ = jnp.split(mm(h_l, p['w_in']), pts, axis=-1)
    pc = jnp.split(mm(h_c, p['w_in']), pts, axis=-1)
    y_ssd_l, y_ssd_c = ssd_mixer(pl_[0:4], pc[0:4], p, with_ctx_out)
    y_swa_l, y_swa_c = swa_mixer(pl_[4:7], pc[4:7], p, rope_swa, with_ctx_out)
    y_mla_l, y_mla_c = mla_mixer(pl_[7:10], pc[7:10], p, rope_mla, with_ctx_out)
    b_ga, b_gb, b_gc = jnp.split(p['b_gate'], 3)

    def merge(ys, yw, ym, gates):
        ga = jax.nn.sigmoid(gates[0] + b_ga)
        gb = jax.nn.sigmoid(gates[1] + b_gb)
        gc = jax.nn.sigmoid(gates[2] + b_gc)
        m = ga * mm(ys, p['w_br_ssd']) + gb * mm(yw, p['w_br_swa']) + gc * mm(ym, p['w_br_mla'])
        return mm(m, p['w_out'])

    out_l = merge(y_ssd_l, y_swa_l, y_mla_l, pl_[10:13])
    out_c = merge(y_ssd_c, y_swa_c, y_mla_c, pc[10:13]) if with_ctx_out else None
    return out_l, out_c


PEER_ROUTE_TM = 256
PEER_TM = 512
PEER_EC = 512
PEER_VMEM_LIMIT = 56 * 1024 * 1024
_SQRT_HALF = 0.7071067811865476


def _top16_rows(s):
    rows = []
    for _ in range(PEER_TOPK):
        m = jnp.max(s, axis=0, keepdims=True)
        rows.append(m)
        s = jnp.where(s == m, -jnp.inf, s)
    return jnp.concatenate(rows, axis=0)


def _peer_route_kernel(h_ref, wq_ref, keys_ref, s0_ref, s1_ref, a_ref, b_ref, tau_ref, q_scr):
    f32 = jnp.float32
    q = jnp.dot(h_ref[...].astype(jnp.bfloat16), wq_ref[...], preferred_element_type=f32)
    q_scr[...] = q.astype(jnp.bfloat16)

    def head_body(hd, carry):
        def scores(p):
            col = pl.multiple_of((hd * 2 + p) * PEER_HALF, PEER_HALF)
            qs = q_scr[:, pl.ds(col, PEER_HALF)]
            return lax.dot_general(keys_ref[p * PEER_HEADS + hd], qs, (((1,), (1,)), ((), ())),
                                   preferred_element_type=f32)

        s0 = scores(0)
        s1 = scores(1)
        a_top = _top16_rows(s0)
        b_top = _top16_rows(s1)
        cand = [a_top[0:1] + b_top]
        cand += [a_top[r:r + 1] + b_top[0:8] for r in range(1, 8)]
        cand.append(a_top[8:16] + b_top[0:1])
        best = _top16_rows(jnp.concatenate(cand, axis=0))
        z = jnp.sum(jnp.exp(best - best[0:1]), axis=0, keepdims=True)
        s0_ref[hd] = s0
        s1_ref[hd] = s1
        a_ref[hd] = jnp.exp(s0 - a_top[0:1])
        b_ref[hd] = jnp.exp(s1 - b_top[0:1]) / z
        tau_ref[hd] = best[PEER_TOPK - 1:PEER_TOPK]
        return carry

    lax.fori_loop(0, PEER_HEADS, head_body, 0)


def _gelu_exact(x):
    return 0.5 * x * (1.0 + lax.erf(x * _SQRT_HALF))


def _peer_expert_kernel(h_ref, u_ref, vt_ref, s0_ref, s1_ref, a_ref, b_ref, tau_ref, o_ref, acc_ref, g_scr):
    f32 = jnp.float32
    j = pl.program_id(1)
    rows_per_step = PEER_EC // PEER_N_KEYS

    @pl.when(j == 0)
    def _():
        acc_ref[...] = jnp.zeros_like(acc_ref)

    act = lax.dot_general(u_ref[...], h_ref[...], (((1,), (1,)), ((), ())),
                          preferred_element_type=f32)
    for ii in range(rows_per_step):
        i = j * rows_per_step + ii
        w = jnp.zeros((PEER_N_KEYS, PEER_TM), f32)
        for hd in range(PEER_HEADS):
            s0_row = s0_ref[hd, pl.ds(i, 1), :]
            a_row = a_ref[hd, pl.ds(i, 1), :]
            keep = (s1_ref[hd] + s0_row) >= tau_ref[hd]
            w = w + jnp.where(keep, b_ref[hd], 0.0) * a_row
        blk = slice(ii * PEER_N_KEYS, (ii + 1) * PEER_N_KEYS)
        g_scr[blk, :] = (w * _gelu_exact(act[blk, :])).astype(jnp.bfloat16)
    acc_ref[...] += jnp.dot(vt_ref[...], g_scr[...], preferred_element_type=f32)

    @pl.when(j == pl.num_programs(1) - 1)
    def _():
        o_ref[...] = acc_ref[...].T


def peer_ffn(h, w_q, sub_keys, u_tab, v_tab):
    bsz, T, D = h.shape
    n_tok = bsz * T
    n_exp = u_tab.shape[0]
    hf = h.reshape(n_tok, D)
    f32 = jnp.float32
    bf16 = jnp.bfloat16
    keys = sub_keys.reshape(2 * PEER_HEADS, PEER_N_KEYS, PEER_HALF).astype(bf16)
    tm = PEER_ROUTE_TM
    big = jax.ShapeDtypeStruct((PEER_HEADS, PEER_N_KEYS, n_tok), f32)
    big_spec = pl.BlockSpec((PEER_HEADS, PEER_N_KEYS, tm), lambda t: (0, 0, t))
    s0, s1, a, b, tau = pl.pallas_call(
        _peer_route_kernel,
        grid=(n_tok // tm,),
        in_specs=[pl.BlockSpec((tm, D), lambda t: (t, 0)),
                  pl.BlockSpec((D, PEER_HEADS * PEER_D_KEY), lambda t: (0, 0)),
                  pl.BlockSpec((2 * PEER_HEADS, PEER_N_KEYS, PEER_HALF), lambda t: (0, 0, 0))],
        out_specs=[big_spec, big_spec, big_spec, big_spec,
                   pl.BlockSpec((PEER_HEADS, 1, tm), lambda t: (0, 0, t))],
        out_shape=[big, big, big, big, jax.ShapeDtypeStruct((PEER_HEADS, 1, n_tok), f32)],
        scratch_shapes=[pltpu.VMEM((tm, PEER_HEADS * PEER_D_KEY), bf16)],
        compiler_params=pltpu.CompilerParams(dimension_semantics=("parallel",),
                                             vmem_limit_bytes=PEER_VMEM_LIMIT),
        name="peer_route",
    )(hf, w_q.astype(bf16), keys)

    tm = PEER_TM
    ec = PEER_EC
    big_spec = pl.BlockSpec((PEER_HEADS, PEER_N_KEYS, tm), lambda t, j: (0, 0, t))
    out = pl.pallas_call(
        _peer_expert_kernel,
        grid=(n_tok // tm, n_exp // ec),
        in_specs=[pl.BlockSpec((tm, D), lambda t, j: (t, 0)),
                  pl.BlockSpec((ec, D), lambda t, j: (j, 0)),
                  pl.BlockSpec((D, ec), lambda t, j: (0, j)),
                  big_spec, big_spec, big_spec, big_spec,
                  pl.BlockSpec((PEER_HEADS, 1, tm), lambda t, j: (0, 0, t))],
        out_specs=pl.BlockSpec((tm, D), lambda t, j: (t, 0)),
        out_shape=jax.ShapeDtypeStruct((n_tok, D), f32),
        scratch_shapes=[pltpu.VMEM((D, tm), f32), pltpu.VMEM((ec, tm), bf16)],
        compiler_params=pltpu.CompilerParams(dimension_semantics=("parallel", "arbitrary"),
                                             vmem_limit_bytes=PEER_VMEM_LIMIT),
        name="peer_expert",
    )(hf.astype(bf16), u_tab.astype(bf16), v_tab.T.astype(bf16), s0, s1, a, b, tau)
    return out.reshape(bsz, T, D)


def layer(xl, xc, c, c_ctx, p, rope_swa, rope_mla, with_ctx_out):
    mod_l = (jax.nn.silu(c) @ p['w_mod'] + p['b_mod'])[:, None, :]
    mod_c = (jax.nn.silu(c_ctx) @ p['w_mod'] + p['b_mod'])[None, None, :]
    sh1_l, sc1_l, g1_l, sh2_l, sc2_l, g2_l = jnp.split(mod_l, 6, axis=-1)
    sh1_c, sc1_c, g1_c, sh2_c, sc2_c, g2_c = jnp.split(mod_c, 6, axis=-1)
    h_l = rms_norm(xl, p['norm1_g']) * (1 + sc1_l) + sh1_l
    h_c = rms_norm(xc, p['norm1_g']) * (1 + sc1_c) + sh1_c
    mix_l, mix_c = token_mixers(h_l, h_c, p, rope_swa, rope_mla, with_ctx_out)
    xl = xl + g1_l * mix_l
    h2_l = rms_norm(xl, p['norm2_g']) * (1 + sc2_l) + sh2_l
    if with_ctx_out:
        xc = xc + g1_c * mix_c
        h2_c = rms_norm(xc, p['norm2_g']) * (1 + sc2_c) + sh2_c
        n_ctx = xc.shape[1]
        f = peer_ffn(jnp.concatenate([h2_c, h2_l], axis=1), p['peer_w_q'], p['peer_keys'], p['peer_u'], p['peer_v'])
        f_c, f_l = f[:, :n_ctx], f[:, n_ctx:]
        xc = xc + g2_c * f_c
    else:
        f_l = peer_ffn(h2_l, p['peer_w_q'], p['peer_keys'], p['peer_u'], p['peer_v'])
    xl = xl + g2_l * f_l
    return xl, xc


def kernel(x, c, ctx, c_ctx, w_mod, b_mod, norm1_g, norm2_g, w_in, ssd_conv_w, ssd_conv_b, ssd_dt_bias,
           ssd_a_log, ssd_d, ssd_norm_g, swa_q_norm, swa_k_norm, swa_sink, mla_q_a_norm, mla_kv_a_norm,
           mla_w_uq, mla_w_ukv, mla_q_norm, mla_k_norm, b_gate, w_br_ssd, w_br_swa, w_br_mla, w_out,
           peer_w_q, peer_keys, peer_u, peer_v):
    L = x.shape[1]
    rows = L // GRID_W
    rope_swa = axial_rope(rows, SWA_HEAD_DIM)
    rope_mla = axial_rope(rows, MLA_ROPE)
    params = dict(w_mod=w_mod, b_mod=b_mod, norm1_g=norm1_g, norm2_g=norm2_g, w_in=w_in,
                  ssd_conv_w=ssd_conv_w, ssd_conv_b=ssd_conv_b, ssd_dt_bias=ssd_dt_bias,
                  ssd_a_log=ssd_a_log, ssd_d=ssd_d, ssd_norm_g=ssd_norm_g, swa_q_norm=swa_q_norm,
                  swa_k_norm=swa_k_norm, swa_sink=swa_sink, mla_q_a_norm=mla_q_a_norm,
                  mla_kv_a_norm=mla_kv_a_norm, mla_w_uq=mla_w_uq, mla_w_ukv=mla_w_ukv,
                  mla_q_norm=mla_q_norm, mla_k_norm=mla_k_norm, b_gate=b_gate, w_br_ssd=w_br_ssd,
                  w_br_swa=w_br_swa, w_br_mla=w_br_mla, w_out=w_out, peer_w_q=peer_w_q,
                  peer_keys=peer_keys, peer_u=peer_u, peer_v=peer_v)
    xl, xc = x, ctx
    for i in range(DEPTH):
        p = {k: v[i] for k, v in params.items()}
        xl, xc = layer(xl, xc, c, c_ctx, p, rope_swa, rope_mla, i < DEPTH - 1)
    return xl
```

```python
import functools

import jax
import jax.numpy as jnp
from jax import lax
from jax.experimental import pallas as pl
from jax.experimental.pallas import tpu as pltpu

D_MODEL = 1024
DEPTH = 2
GRID_W = 64
EPS = 1e-6
NEG_INF = -1e30
ROPE_BASE = 10000.0

SSD_D_INNER = 1024
SSD_HEAD_DIM = 64
SSD_HEADS = 16
SSD_GROUPS = 2
SSD_HEADS_PER_GROUP = 8
SSD_STATE = 64
SSD_CONV = 5
SSD_CHUNK = 128
SSD_CONV_CH = 1280

SWA_HEADS = 8
SWA_KV_HEADS = 2
SWA_HEAD_DIM = 64
SWA_WINDOW = 128
SWA_BLOCK = 128
SWA_SCALE = SWA_HEAD_DIM ** -0.5

MLA_HEADS = 8
MLA_Q_RANK = 384
MLA_KV_RANK = 256
MLA_NOPE = 64
MLA_ROPE = 32
MLA_V = 64
MLA_QK = MLA_NOPE + MLA_ROPE
MLA_BLOCK = 128
MLA_SCALE = MLA_QK ** -0.5
MLA_TQ = 512
MLA_TK = 768

PEER_HEADS = 8
PEER_N_KEYS = 128
PEER_D_KEY = 256
PEER_HALF = 128
PEER_TOPK = 16
PEER_BLOCK = 64

IN_SIZES = (SSD_D_INNER, SSD_CONV_CH, SSD_HEADS, SSD_HEADS,
            SWA_HEADS * SWA_HEAD_DIM, SWA_KV_HEADS * SWA_HEAD_DIM, SWA_KV_HEADS * SWA_HEAD_DIM,
            MLA_Q_RANK, MLA_KV_RANK, MLA_ROPE,
            D_MODEL, D_MODEL, D_MODEL)
IN_WIDTH = sum(IN_SIZES)

LANES = 128


def _round_up(n, m):
    return (n + m - 1) // m * m


def _mm_kernel(a_ref, b_ref, o_ref):
    o_ref[...] = jnp.dot(a_ref[...].astype(jnp.bfloat16), b_ref[...],
                         preferred_element_type=jnp.float32)


def _pick_tile(n, cap, unit):
    t = min(n, cap)
    while n % t or t % unit:
        t -= unit
    return t


def pmatmul(a, w):
    m, k = a.shape
    n = w.shape[1]
    n_pad = _round_up(n, LANES)
    wb = w.astype(jnp.bfloat16)
    if n_pad != n:
        wb = jnp.pad(wb, ((0, 0), (0, n_pad - n)))
    tm = _pick_tile(m, 512, 8)
    tn = _pick_tile(n_pad, 1024, LANES)
    out = pl.pallas_call(
        _mm_kernel,
        grid=(m // tm, n_pad // tn),
        in_specs=[pl.BlockSpec((tm, k), lambda i, j: (i, 0)),
                  pl.BlockSpec((k, tn), lambda i, j: (0, j))],
        out_specs=pl.BlockSpec((tm, tn), lambda i, j: (i, j)),
        out_shape=jax.ShapeDtypeStruct((m, n_pad), jnp.float32),
        compiler_params=pltpu.CompilerParams(dimension_semantics=("parallel", "parallel")),
        name="matmul",
    )(a, wb)
    return out[:, :n] if n_pad != n else out


def mm(a, w):
    lead = a.shape[:-1]
    return pmatmul(a.reshape(-1, a.shape[-1]), w).reshape(lead + (w.shape[1],))


def _flash_kernel(q_ref, k_ref, v_ref, o_ref, *, tk):
    f32 = jnp.float32
    q = q_ref[0, 0]
    tq = q.shape[0]
    dv = v_ref.shape[-1]
    n_chunks = k_ref.shape[2] // tk

    def body(c, carry):
        m, l, acc = carry
        off = pl.multiple_of(c * tk, tk)
        k = k_ref[0, 0, pl.ds(off, tk), :]
        v = v_ref[0, 0, pl.ds(off, tk), :]
        s = lax.dot_general(q, k, (((1,), (1,)), ((), ())), preferred_element_type=f32)
        m_new = jnp.maximum(m, jnp.max(s, axis=-1, keepdims=True))
        p = jnp.exp(s - m_new)
        alpha = jnp.exp(m - m_new)
        l = alpha * l + jnp.sum(p, axis=-1, keepdims=True)
        acc = alpha * acc + jnp.dot(p.astype(jnp.bfloat16), v, preferred_element_type=f32)
        return m_new, l, acc

    init = (jnp.full((tq, 1), -jnp.inf, f32), jnp.zeros((tq, 1), f32), jnp.zeros((tq, dv), f32))
    _, l, acc = lax.fori_loop(0, n_chunks, body, init, unroll=True)
    o_ref[0, 0] = acc / l


def flash_attention(q, k, v, tq, tk):
    bsz, nh, lq, dq = q.shape
    nk, dv = v.shape[2], v.shape[3]
    bf16 = jnp.bfloat16
    return pl.pallas_call(
        functools.partial(_flash_kernel, tk=tk),
        grid=(bsz, nh, lq // tq),
        in_specs=[pl.BlockSpec((1, 1, tq, dq), lambda b, h, i: (b, h, i, 0)),
                  pl.BlockSpec((1, 1, nk, dq), lambda b, h, i: (b, h, 0, 0)),
                  pl.BlockSpec((1, 1, nk, dv), lambda b, h, i: (b, h, 0, 0))],
        out_specs=pl.BlockSpec((1, 1, tq, dv), lambda b, h, i: (b, h, i, 0)),
        out_shape=jax.ShapeDtypeStruct((bsz, nh, lq, dv), jnp.float32),
        compiler_params=pltpu.CompilerParams(dimension_semantics=("parallel", "parallel", "parallel"),
                                             vmem_limit_bytes=PEER_VMEM_LIMIT),
        name="flash_attention",
    )(q.astype(bf16), k.astype(bf16), v.astype(bf16))


SWA_GROUP = SWA_HEADS // SWA_KV_HEADS
SWA_BAND = 3 * SWA_BLOCK


def _swa_kernel(q_ref, k_ref, v_ref, kc_ref, vc_ref, sink_ref, o_ref):
    f32 = jnp.float32
    bf16 = jnp.bfloat16
    n = pl.program_id(2)
    seq = k_ref.shape[2]
    rows = SWA_GROUP * SWA_BLOCK
    q = q_ref[0].reshape(rows, SWA_HEAD_DIM)
    start = pl.multiple_of(jnp.clip((n - 1) * SWA_BLOCK, 0, seq - SWA_BAND), SWA_BLOCK)
    kb = k_ref[0, 0, pl.ds(start, SWA_BAND), :]
    vb = v_ref[0, 0, pl.ds(start, SWA_BAND), :]
    nt = (((1,), (1,)), ((), ()))
    s_band = lax.dot_general(q, kb, nt, preferred_element_type=f32)
    s_ctx = lax.dot_general(q, kc_ref[0, 0], nt, preferred_element_type=f32)
    qpos = n * SWA_BLOCK + lax.broadcasted_iota(jnp.int32, (rows, SWA_BAND), 0) % SWA_BLOCK
    kpos = start + lax.broadcasted_iota(jnp.int32, (rows, SWA_BAND), 1)
    s_band = jnp.where(jnp.abs(kpos - qpos) <= SWA_WINDOW, s_band, NEG_INF)
    sink = sink_ref[0]
    m = jnp.maximum(jnp.maximum(jnp.max(s_band, axis=-1, keepdims=True),
                                jnp.max(s_ctx, axis=-1, keepdims=True)), sink)
    p_band = jnp.exp(s_band - m)
    p_ctx = jnp.exp(s_ctx - m)
    denom = (jnp.sum(p_band, axis=-1, keepdims=True) + jnp.sum(p_ctx, axis=-1, keepdims=True)
             + jnp.exp(sink - m))
    o = (jnp.dot(p_band.astype(bf16), vb, preferred_element_type=f32)
         + jnp.dot(p_ctx.astype(bf16), vc_ref[0, 0], preferred_element_type=f32))
    o_ref[0] = (o / denom).reshape(SWA_GROUP, SWA_BLOCK, SWA_HEAD_DIM)


def swa_attention(q, k, v, kc, vc, sink):
    bsz, _, seq, hd = q.shape
    n_ctx = kc.shape[2]
    bf16 = jnp.bfloat16
    sink_rows = jnp.repeat(sink.astype(jnp.float32), SWA_BLOCK).reshape(SWA_KV_HEADS, SWA_GROUP * SWA_BLOCK, 1)
    kv_spec = pl.BlockSpec((1, 1, seq, hd), lambda b, h, n: (b, h, 0, 0))
    ctx_spec = pl.BlockSpec((1, 1, n_ctx, hd), lambda b, h, n: (b, h, 0, 0))
    return pl.pallas_call(
        _swa_kernel,
        grid=(bsz, SWA_KV_HEADS, seq // SWA_BLOCK),
        in_specs=[pl.BlockSpec((1, SWA_GROUP, SWA_BLOCK, hd), lambda b, h, n: (b, h, n, 0)),
                  kv_spec, kv_spec, ctx_spec, ctx_spec,
                  pl.BlockSpec((1, SWA_GROUP * SWA_BLOCK, 1), lambda b, h, n: (h, 0, 0))],
        out_specs=pl.BlockSpec((1, SWA_GROUP, SWA_BLOCK, hd), lambda b, h, n: (b, h, n, 0)),
        out_shape=jax.ShapeDtypeStruct((bsz, SWA_HEADS, seq, hd), jnp.float32),
        compiler_params=pltpu.CompilerParams(dimension_semantics=("parallel", "parallel", "parallel")),
        name="swa_attention",
    )(q.astype(bf16), k.astype(bf16), v.astype(bf16), kc.astype(bf16), vc.astype(bf16), sink_rows)


SSD_PAIR = 2 * SSD_HEAD_DIM


def _softplus(x):
    return jnp.maximum(x, 0.0) + jnp.log1p(jnp.exp(-jnp.abs(x)))


def _ssd_kernel(u_ref, dt_ref, dtt_ref, bias_r_ref, bias_c_ref, alog_r_ref, alog_c_ref, dx_ref,
                y_ref, state_ref, *, reverse):
    f32 = jnp.float32
    bf16 = jnp.bfloat16
    q = SSD_CHUNK
    hi = lax.Precision.HIGHEST

    @pl.when(pl.program_id(1) == 0)
    def _():
        state_ref[...] = jnp.zeros_like(state_ref)

    row = lax.broadcasted_iota(jnp.int32, (q, q), 0)
    col = lax.broadcasted_iota(jnp.int32, (q, q), 1)
    before = (col >= row) if reverse else (col <= row)
    tri = before.astype(f32)

    dt = _softplus(dt_ref[0] + bias_r_ref[...])
    dtt = _softplus(dtt_ref[0] + bias_c_ref[...])
    a_r = -jnp.exp(alog_r_ref[...])
    a_c = -jnp.exp(alog_c_ref[...])
    cum = jnp.dot(tri, dt * a_r, precision=hi, preferred_element_type=f32)
    cumt = lax.dot_general(dtt * a_c, tri, (((1,), (1,)), ((), ())), precision=hi,
                           preferred_element_type=f32)
    tot = cum[0:1, :] if reverse else cum[q - 1:q, :]
    ecum = jnp.exp(cum)
    wgt = jnp.exp(tot - cum) * dt
    etot = jnp.exp(tot)

    lane_lo = lax.broadcasted_iota(jnp.int32, (q, SSD_PAIR), 1) < SSD_HEAD_DIM
    lane_lo_row = lane_lo[0:1, :]
    heads_per_group = SSD_HEADS_PER_GROUP
    for g in range(SSD_GROUPS):
        b_off = SSD_D_INNER + g * SSD_STATE
        c_off = SSD_D_INNER + SSD_GROUPS * SSD_STATE + g * SSD_STATE
        bg = u_ref[0, :, b_off:b_off + SSD_STATE].astype(bf16)
        cg = u_ref[0, :, c_off:c_off + SSD_STATE].astype(bf16)
        cb = lax.dot_general(cg, bg, (((1,), (1,)), ((), ())), preferred_element_type=f32)
        for kk in range(heads_per_group // 2):
            k = g * (heads_per_group // 2) + kk
            lanes = slice(k * SSD_PAIR, (k + 1) * SSD_PAIR)
            xp = u_ref[0, :, lanes]
            y = dx_ref[:, lanes] * xp
            for half in range(2):
                hh = 2 * k + half
                seg = cum[:, hh:hh + 1] - cumt[hh:hh + 1, :]
                decay = jnp.where(before, jnp.exp(seg), 0.0) * dtt[hh:hh + 1, :]
                m = (cb * decay).astype(bf16)
                xm = jnp.where(lane_lo if half == 0 else jnp.logical_not(lane_lo), xp, 0.0).astype(bf16)
                y = y + jnp.dot(m, xm, preferred_element_type=f32)
            e_pair = jnp.where(lane_lo, ecum[:, 2 * k:2 * k + 1], ecum[:, 2 * k + 1:2 * k + 2])
            w_pair = jnp.where(lane_lo, wgt[:, 2 * k:2 * k + 1], wgt[:, 2 * k + 1:2 * k + 2])
            t_pair = jnp.where(lane_lo_row, etot[:, 2 * k:2 * k + 1], etot[:, 2 * k + 1:2 * k + 2])
            prev = state_ref[:, lanes]
            y = y + jnp.dot(cg, prev.astype(bf16), preferred_element_type=f32) * e_pair
            y_ref[0, :, lanes] = y
            xw = (xp * w_pair).astype(bf16)
            s_new = lax.dot_general(bg, xw, (((0,), (0,)), ((), ())), preferred_element_type=f32)
            state_ref[:, lanes] = prev * t_pair + s_new


def ssd_scan_dir(u, dt_raw, bias, a_log, d_skip, reverse, n_ctx):
    bsz, t_all, _ = u.shape
    q = SSD_CHUNK
    nc = t_all // q
    ncc = n_ctx // q
    if reverse:
        def chunk(c):
            return jnp.where(c < ncc, ncc - 1 - c, nc + ncc - 1 - c)
    else:
        def chunk(c):
            return c
    f32 = jnp.float32
    row = lambda v: v.astype(f32).reshape(1, SSD_HEADS)
    colv = lambda v: v.astype(f32).reshape(SSD_HEADS, 1)
    small_r = pl.BlockSpec((1, SSD_HEADS), lambda b, c: (0, 0))
    small_c = pl.BlockSpec((SSD_HEADS, 1), lambda b, c: (0, 0))
    return pl.pallas_call(
        functools.partial(_ssd_kernel, reverse=reverse),
        grid=(bsz, nc),
        in_specs=[pl.BlockSpec((1, q, SSD_CONV_CH), lambda b, c: (b, chunk(c), 0)),
                  pl.BlockSpec((1, q, SSD_HEADS), lambda b, c: (b, chunk(c), 0)),
                  pl.BlockSpec((1, SSD_HEADS, q), lambda b, c: (b, 0, chunk(c))),
                  small_r, small_c, small_r, small_c,
                  pl.BlockSpec((1, SSD_D_INNER), lambda b, c: (0, 0))],
        out_specs=pl.BlockSpec((1, q, SSD_D_INNER), lambda b, c: (b, chunk(c), 0)),
        out_shape=jax.ShapeDtypeStruct((bsz, t_all, SSD_D_INNER), f32),
        scratch_shapes=[pltpu.VMEM((SSD_STATE, SSD_D_INNER), f32)],
        compiler_params=pltpu.CompilerParams(dimension_semantics=("parallel", "arbitrary")),
        name="ssd_scan_bwd" if reverse else "ssd_scan_fwd",
    )(u, dt_raw, jnp.swapaxes(dt_raw, 1, 2), row(bias), colv(bias), row(a_log), colv(a_log),
      jnp.repeat(d_skip.astype(f32), SSD_HEAD_DIM).reshape(1, SSD_D_INNER))


def split_points():
    pts, acc = [], 0
    for s in IN_SIZES[:-1]:
        acc += s
        pts.append(acc)
    return pts


def rms_norm(x, g):
    xf = x.astype(jnp.float32)
    y = xf * lax.rsqrt(jnp.mean(xf * xf, axis=-1, keepdims=True) + EPS)
    return (y * g.astype(jnp.float32)).astype(x.dtype)


def axial_rope(rows, dim):
    pairs = dim // 4
    freqs = ROPE_BASE ** (-jnp.arange(pairs, dtype=jnp.float32) / pairs)
    pos_r = jnp.repeat(jnp.arange(rows, dtype=jnp.float32), GRID_W)
    pos_c = jnp.tile(jnp.arange(GRID_W, dtype=jnp.float32), rows)
    ang = jnp.concatenate([pos_r[:, None] * freqs, pos_c[:, None] * freqs], axis=-1)
    return jnp.cos(ang), jnp.sin(ang)


def apply_rope(x, cos, sin):
    d2 = x.shape[-1] // 2
    x1, x2 = x[..., :d2], x[..., d2:]
    cs, sn = cos[:, None, :], sin[:, None, :]
    return jnp.concatenate([x1 * cs - x2 * sn, x2 * cs + x1 * sn], axis=-1).astype(x.dtype)


def dwconv_centred(u, w, b):
    pad = (SSD_CONV - 1) // 2
    y = lax.conv_general_dilated(u, w[:, None, :].astype(u.dtype), window_strides=(1,),
                                 padding=[(pad, pad)], dimension_numbers=('NWC', 'WIO', 'NWC'),
                                 feature_group_count=u.shape[-1])
    return y + b


def ssd_scan(xh, dt, bm, cm, a, d_skip, init_state, with_y):
    f32 = jnp.float32
    bsz, T, G, Hg, P = xh.shape
    N = bm.shape[-1]
    Q = SSD_CHUNK
    nc = T // Q
    x_c = xh.reshape(bsz, nc, Q, G, Hg, P)
    dt_c = dt.astype(f32).reshape(bsz, nc, Q, G, Hg)
    b_c = bm.reshape(bsz, nc, Q, G, N)
    c_c = cm.reshape(bsz, nc, Q, G, N)
    cum = jnp.cumsum(dt_c * a, axis=2)
    decay_to_end = jnp.exp(cum[:, :, -1:] - cum)
    states = jnp.einsum('bcjgn,bcjgh,bcjghp->bcghpn', b_c, decay_to_end * dt_c, x_c)
    chunk_decay = jnp.exp(cum[:, :, -1])

    def step(s, inp):
        st, dec = inp
        return s * dec[..., None, None] + st, s

    final, prev = lax.scan(step, init_state.astype(f32),
                           (jnp.moveaxis(states, 1, 0), jnp.moveaxis(chunk_decay, 1, 0)))
    if not with_y:
        return None, final
    prev = jnp.moveaxis(prev, 0, 1)
    cum_t = jnp.moveaxis(cum, 2, -1)
    seg = cum_t[..., :, None] - cum_t[..., None, :]
    mask = jnp.tril(jnp.ones((Q, Q), dtype=bool))
    decay = jnp.where(mask, jnp.exp(jnp.where(mask, seg, 0.0)), 0.0)
    cb = jnp.einsum('bcign,bcjgn->bcgij', c_c, b_c)
    y_diag = jnp.einsum('bcgij,bcghij,bcjgh,bcjghp->bcighp', cb, decay, dt_c, x_c)
    y_off = jnp.einsum('bcign,bcghpn,bcigh->bcighp', c_c, prev, jnp.exp(cum))
    y = y_diag + y_off + d_skip[:, :, None] * x_c
    return y.reshape(bsz, T, G, Hg, P), final


def ssd_direction(ctx_in, lat_in, a, d_skip, reverse, with_ctx_out):
    if reverse:
        ctx_in = [jnp.flip(t, 1) for t in ctx_in]
        lat_in = [jnp.flip(t, 1) for t in lat_in]
    s0 = jnp.zeros((lat_in[0].shape[0], SSD_GROUPS, SSD_HEADS_PER_GROUP, SSD_HEAD_DIM, SSD_STATE), jnp.float32)
    y_c, s_c = ssd_scan(*ctx_in, a, d_skip, s0, with_ctx_out)
    y_l, _ = ssd_scan(*lat_in, a, d_skip, s_c, True)
    if reverse:
        y_l = jnp.flip(y_l, 1)
        if with_ctx_out:
            y_c = jnp.flip(y_c, 1)
    return y_l, y_c


def ssd_mixer(lat, ctx, p, with_ctx_out):
    f32 = jnp.float32

    def conv_act(xbc):
        return jax.nn.silu(dwconv_centred(xbc, p['ssd_conv_w'], p['ssd_conv_b']))

    n_ctx = ctx[1].shape[1]
    u_all = jnp.concatenate([conv_act(ctx[1]), conv_act(lat[1])], axis=1)
    y_all = None
    for direction in range(2):
        dt_raw = jnp.concatenate([ctx[2 + direction], lat[2 + direction]], axis=1)
        y_d = ssd_scan_dir(u_all, dt_raw, p['ssd_dt_bias'][direction], p['ssd_a_log'][direction],
                           p['ssd_d'][direction], direction == 1, n_ctx)
        y_all = y_d if y_all is None else y_all + y_d

    def gated_out(y, z):
        return rms_norm(y * jax.nn.silu(z.astype(f32)), p['ssd_norm_g']).astype(z.dtype)

    out_l = gated_out(y_all[:, n_ctx:], lat[0])
    out_c = gated_out(y_all[:, :n_ctx], ctx[0]) if with_ctx_out else None
    return out_l, out_c


def swa_mixer(lat, ctx, p, rope, with_ctx_out):
    f32 = jnp.float32
    grp = SWA_HEADS // SWA_KV_HEADS

    def heads(q, k, v):
        bsz, T, _ = q.shape
        q = rms_norm(q.reshape(bsz, T, SWA_HEADS, SWA_HEAD_DIM), p['swa_q_norm'])
        k = rms_norm(k.reshape(bsz, T, SWA_KV_HEADS, SWA_HEAD_DIM), p['swa_k_norm'])
        return q, k, v.reshape(bsz, T, SWA_KV_HEADS, SWA_HEAD_DIM)

    ql, kl, vl = heads(*lat)
    qc, kc, vc = heads(*ctx)
    cos, sin = rope
    ql = apply_rope(ql, cos, sin) * SWA_SCALE
    kl = apply_rope(kl, cos, sin)
    bsz, L = ql.shape[:2]
    sink = p['swa_sink'].astype(f32).reshape(SWA_KV_HEADS, grp)
    hm = lambda t: jnp.swapaxes(t, 1, 2)
    o = swa_attention(hm(ql), hm(kl), hm(vl), hm(kc), hm(vc), p['swa_sink'])
    y_l = hm(o).reshape(bsz, L, SWA_HEADS * SWA_HEAD_DIM)

    y_c = None
    if with_ctx_out:
        qcb = (qc * SWA_SCALE).reshape(bsz, -1, SWA_KV_HEADS, grp, SWA_HEAD_DIM)
        s = jnp.einsum('bqkgd,bjkd->bkgqj', qcb, kc).astype(f32)
        sink_c = jnp.broadcast_to(sink[None, :, :, None, None], s.shape[:-1] + (1,))
        pc = jax.nn.softmax(jnp.concatenate([s, sink_c], axis=-1), axis=-1)[..., :-1].astype(vc.dtype)
        y_c = jnp.einsum('bkgqj,bjkd->bqkgd', pc, vc).reshape(bsz, -1, SWA_HEADS * SWA_HEAD_DIM)
    return y_l, y_c


def mla_mixer(lat, ctx, p, rope, with_ctx_out):
    f32 = jnp.float32

    def heads(cq, ckv, kr):
        bsz, T, _ = cq.shape
        q = mm(rms_norm(cq, p['mla_q_a_norm']), p['mla_w_uq']).reshape(bsz, T, MLA_HEADS, MLA_QK)
        kv = mm(rms_norm(ckv, p['mla_kv_a_norm']), p['mla_w_ukv']).reshape(bsz, T, MLA_HEADS, MLA_NOPE + MLA_V)
        k = jnp.concatenate([kv[..., :MLA_NOPE],
                             jnp.broadcast_to(kr[:, :, None, :], (bsz, T, MLA_HEADS, MLA_ROPE))], axis=-1)
        return rms_norm(q, p['mla_q_norm']), rms_norm(k, p['mla_k_norm']), kv[..., MLA_NOPE:]

    cos, sin = rope

    def rope_tail(t):
        return jnp.concatenate([t[..., :MLA_NOPE], apply_rope(t[..., MLA_NOPE:], cos, sin)], axis=-1)

    ql, kl, vl = heads(*lat)
    qc, kc, vc = heads(*ctx)
    ql = rope_tail(ql) * MLA_SCALE
    kl = rope_tail(kl)
    bsz, L = ql.shape[:2]
    nb = L // MLA_BLOCK
    k_all = jnp.concatenate([kc, kl], axis=1)
    v_all = jnp.concatenate([vc, vl], axis=1)

    def attend(qb, k, v, tq, tk):
        hm = lambda t: jnp.swapaxes(t, 1, 2)
        o = flash_attention(hm(qb), hm(k), hm(v), tq, tk)
        return hm(o).reshape(qb.shape[0], qb.shape[1], MLA_HEADS * MLA_V)

    y_l = attend(ql, k_all, v_all, MLA_TQ, MLA_TK)
    y_c = None
    if with_ctx_out:
        n_ctx = qc.shape[1]
        y_c = attend(qc * MLA_SCALE, kc, vc, n_ctx, n_ctx)
    return y_l, y_c


def token_mixers(h_l, h_c, p, rope_swa, rope_mla, with_ctx_out):
    pts = split_points()
    pl_ = jnp.split(mm(h_l, p['w_in']), pts, axis=-1)
    pc = jnp.split(mm(h_c, p['w_in']), pts, axis=-1)
    y_ssd_l, y_ssd_c = ssd_mixer(pl_[0:4], pc[0:4], p, with_ctx_out)
    y_swa_l, y_swa_c = swa_mixer(pl_[4:7], pc[4:7], p, rope_swa, with_ctx_out)
    y_mla_l, y_mla_c = mla_mixer(pl_[7:10], pc[7:10], p, rope_mla, with_ctx_out)
    b_ga, b_gb, b_gc = jnp.split(p['b_gate'], 3)

    def merge(ys, yw, ym, gates):
        ga = jax.nn.sigmoid(gates[0] + b_ga)
        gb = jax.nn.sigmoid(gates[1] + b_gb)
        gc = jax.nn.sigmoid(gates[2] + b_gc)
        m = ga * mm(ys, p['w_br_ssd']) + gb * mm(yw, p['w_br_swa']) + gc * mm(ym, p['w_br_mla'])
        return mm(m, p['w_out'])

    out_l = merge(y_ssd_l, y_swa_l, y_mla_l, pl_[10:13])
    out_c = merge(y_ssd_c, y_swa_c, y_mla_c, pc[10:13]) if with_ctx_out else None
    return out_l, out_c


PEER_ROUTE_TM = 256
PEER_TM = 512
PEER_EC = 512
PEER_WL = 128
PEER_VMEM_LIMIT = 56 * 1024 * 1024
_SQRT_HALF = 0.7071067811865476


def _top16_rows(s):
    rows = []
    for _ in range(PEER_TOPK):
        m = jnp.max(s, axis=0, keepdims=True)
        rows.append(m)
        s = jnp.where(s == m, -jnp.inf, s)
    return jnp.concatenate(rows, axis=0)


def _peer_route_kernel(h_ref, wq_ref, keys_ref, s0_ref, s1_ref, a_ref, b_ref, tau_ref, q_scr):
    f32 = jnp.float32
    q = jnp.dot(h_ref[...].astype(jnp.bfloat16), wq_ref[...], preferred_element_type=f32)
    q_scr[...] = q.astype(jnp.bfloat16)

    def head_body(hd, carry):
        def scores(p):
            col = pl.multiple_of((hd * 2 + p) * PEER_HALF, PEER_HALF)
            qs = q_scr[:, pl.ds(col, PEER_HALF)]
            return lax.dot_general(keys_ref[p * PEER_HEADS + hd], qs, (((1,), (1,)), ((), ())),
                                   preferred_element_type=f32)

        s0 = scores(0)
        s1 = scores(1)
        a_top = _top16_rows(s0)
        b_top = _top16_rows(s1)
        cand = [a_top[0:1] + b_top]
        cand += [a_top[r:r + 1] + b_top[0:8] for r in range(1, 8)]
        cand.append(a_top[8:16] + b_top[0:1])
        best = _top16_rows(jnp.concatenate(cand, axis=0))
        z = jnp.sum(jnp.exp(best - best[0:1]), axis=0, keepdims=True)
        a = jnp.exp(s0 - a_top[0:1])
        b = jnp.exp(s1 - b_top[0:1]) / z
        tau = best[PEER_TOPK - 1:PEER_TOPK]
        for tl in range(s0.shape[1] // LANES):
            ts = slice(tl * LANES, (tl + 1) * LANES)
            s0_ref[hd, tl] = s0[:, ts]
            s1_ref[hd, tl] = s1[:, ts]
            a_ref[hd, tl] = a[:, ts]
            b_ref[hd, tl] = b[:, ts]
            tau_ref[hd, tl] = tau[:, ts]
        return carry

    lax.fori_loop(0, PEER_HEADS, head_body, 0)


def _gelu_exact(x):
    return 0.5 * x * (1.0 + lax.erf(x * _SQRT_HALF))


def _peer_expert_kernel(h_ref, u_ref, vt_ref, s0_ref, s1_ref, a_ref, b_ref, tau_ref, o_ref,
                        acc_ref, act_scr, g_scr):
    f32 = jnp.float32
    j = pl.program_id(1)
    rows_per_step = PEER_EC // PEER_N_KEYS

    @pl.when(j == 0)
    def _():
        acc_ref[...] = jnp.zeros_like(acc_ref)

    act_scr[...] = lax.dot_general(u_ref[...], h_ref[...], (((1,), (1,)), ((), ())),
                                   preferred_element_type=f32)
    n_tl = PEER_TM // LANES

    def tile_body(t, carry):
        ii = t // n_tl
        tl = t % n_tl
        i = j * rows_per_step + ii
        w = jnp.zeros((PEER_N_KEYS, LANES), f32)
        for hd in range(PEER_HEADS):
            s0_row = s0_ref[hd, tl, pl.ds(i, 1), :]
            a_row = a_ref[hd, tl, pl.ds(i, 1), :]
            keep = (s1_ref[hd, tl] + s0_row) >= tau_ref[hd, tl]
            w = w + jnp.where(keep, b_ref[hd, tl], 0.0) * a_row
        es = pl.ds(pl.multiple_of(ii * PEER_N_KEYS, PEER_N_KEYS), PEER_N_KEYS)
        ts = pl.ds(pl.multiple_of(tl * LANES, LANES), LANES)
        g_scr[es, ts] = (w * _gelu_exact(act_scr[es, ts])).astype(jnp.bfloat16)
        return carry

    lax.fori_loop(0, rows_per_step * n_tl, tile_body, 0)
    acc_ref[...] += jnp.dot(vt_ref[...], g_scr[...], preferred_element_type=f32)

    @pl.when(j == pl.num_programs(1) - 1)
    def _():
        o_ref[...] = acc_ref[...].T


def peer_ffn(h, w_q, sub_keys, u_tab, v_tab):
    bsz, T, D = h.shape
    n_tok = bsz * T
    n_exp = u_tab.shape[0]
    hf = h.reshape(n_tok, D)
    f32 = jnp.float32
    bf16 = jnp.bfloat16
    keys = sub_keys.reshape(2 * PEER_HEADS, PEER_N_KEYS, PEER_HALF).astype(bf16)
    tm = PEER_ROUTE_TM
    n_tiles = n_tok // LANES
    big = jax.ShapeDtypeStruct((PEER_HEADS, n_tiles, PEER_N_KEYS, LANES), f32)
    small = jax.ShapeDtypeStruct((PEER_HEADS, n_tiles, 1, LANES), f32)
    big_spec = pl.BlockSpec((PEER_HEADS, tm // LANES, PEER_N_KEYS, LANES), lambda t: (0, t, 0, 0))
    small_spec = pl.BlockSpec((PEER_HEADS, tm // LANES, 1, LANES), lambda t: (0, t, 0, 0))
    s0, s1, a, b, tau = pl.pallas_call(
        _peer_route_kernel,
        grid=(n_tok // tm,),
        in_specs=[pl.BlockSpec((tm, D), lambda t: (t, 0)),
                  pl.BlockSpec((D, PEER_HEADS * PEER_D_KEY), lambda t: (0, 0)),
                  pl.BlockSpec((2 * PEER_HEADS, PEER_N_KEYS, PEER_HALF), lambda t: (0, 0, 0))],
        out_specs=[big_spec, big_spec, big_spec, big_spec, small_spec],
        out_shape=[big, big, big, big, small],
        scratch_shapes=[pltpu.VMEM((tm, PEER_HEADS * PEER_D_KEY), bf16)],
        compiler_params=pltpu.CompilerParams(dimension_semantics=("parallel",),
                                             vmem_limit_bytes=PEER_VMEM_LIMIT),
        name="peer_route",
    )(hf, w_q.astype(bf16), keys)

    tm = PEER_TM
    ec = PEER_EC
    big_spec = pl.BlockSpec((PEER_HEADS, tm // LANES, PEER_N_KEYS, LANES), lambda t, j: (0, t, 0, 0))
    small_spec = pl.BlockSpec((PEER_HEADS, tm // LANES, 1, LANES), lambda t, j: (0, t, 0, 0))
    out = pl.pallas_call(
        _peer_expert_kernel,
        grid=(n_tok // tm, n_exp // ec),
        in_specs=[pl.BlockSpec((tm, D), lambda t, j: (t, 0)),
                  pl.BlockSpec((ec, D), lambda t, j: (j, 0)),
                  pl.BlockSpec((D, ec), lambda t, j: (0, j)),
                  big_spec, big_spec, big_spec, big_spec, small_spec],
        out_specs=pl.BlockSpec((tm, D), lambda t, j: (t, 0)),
        out_shape=jax.ShapeDtypeStruct((n_tok, D), f32),
        scratch_shapes=[pltpu.VMEM((D, tm), f32), pltpu.VMEM((ec, tm), f32), pltpu.VMEM((ec, tm), bf16)],
        compiler_params=pltpu.CompilerParams(dimension_semantics=("parallel", "arbitrary"),
                                             vmem_limit_bytes=PEER_VMEM_LIMIT),
        name="peer_expert",
    )(hf.astype(bf16), u_tab.astype(bf16), v_tab.T.astype(bf16), s0, s1, a, b, tau)
    return out.reshape(bsz, T, D)


def layer(xl, xc, c, c_ctx, p, rope_swa, rope_mla, with_ctx_out):
    mod_l = (jax.nn.silu(c) @ p['w_mod'] + p['b_mod'])[:, None, :]
    mod_c = (jax.nn.silu(c_ctx) @ p['w_mod'] + p['b_mod'])[None, None, :]
    sh1_l, sc1_l, g1_l, sh2_l, sc2_l, g2_l = jnp.split(mod_l, 6, axis=-1)
    sh1_c, sc1_c, g1_c, sh2_c, sc2_c, g2_c = jnp.split(mod_c, 6, axis=-1)
    h_l = rms_norm(xl, p['norm1_g']) * (1 + sc1_l) + sh1_l
    h_c = rms_norm(xc, p['norm1_g']) * (1 + sc1_c) + sh1_c
    mix_l, mix_c = token_mixers(h_l, h_c, p, rope_swa, rope_mla, with_ctx_out)
    xl = xl + g1_l * mix_l
    h2_l = rms_norm(xl, p['norm2_g']) * (1 + sc2_l) + sh2_l
    if with_ctx_out:
        xc = xc + g1_c * mix_c
        h2_c = rms_norm(xc, p['norm2_g']) * (1 + sc2_c) + sh2_c
        n_ctx = xc.shape[1]
        f = peer_ffn(jnp.concatenate([h2_c, h2_l], axis=1), p['peer_w_q'], p['peer_keys'], p['peer_u'], p['peer_v'])
        f_c, f_l = f[:, :n_ctx], f[:, n_ctx:]
        xc = xc + g2_c * f_c
    else:
        f_l = peer_ffn(h2_l, p['peer_w_q'], p['peer_keys'], p['peer_u'], p['peer_v'])
    xl = xl + g2_l * f_l
    return xl, xc


def kernel(x, c, ctx, c_ctx, w_mod, b_mod, norm1_g, norm2_g, w_in, ssd_conv_w, ssd_conv_b, ssd_dt_bias,
           ssd_a_log, ssd_d, ssd_norm_g, swa_q_norm, swa_k_norm, swa_sink, mla_q_a_norm, mla_kv_a_norm,
           mla_w_uq, mla_w_ukv, mla_q_norm, mla_k_norm, b_gate, w_br_ssd, w_br_swa, w_br_mla, w_out,
           peer_w_q, peer_keys, peer_u, peer_v):
    L = x.shape[1]
    rows = L // GRID_W
    rope_swa = axial_rope(rows, SWA_HEAD_DIM)
    rope_mla = axial_rope(rows, MLA_ROPE)
    params = dict(w_mod=w_mod, b_mod=b_mod, norm1_g=norm1_g, norm2_g=norm2_g, w_in=w_in,
                  ssd_conv_w=ssd_conv_w, ssd_conv_b=ssd_conv_b, ssd_dt_bias=ssd_dt_bias,
                  ssd_a_log=ssd_a_log, ssd_d=ssd_d, ssd_norm_g=ssd_norm_g, swa_q_norm=swa_q_norm,
                  swa_k_norm=swa_k_norm, swa_sink=swa_sink, mla_q_a_norm=mla_q_a_norm,
                  mla_kv_a_norm=mla_kv_a_norm, mla_w_uq=mla_w_uq, mla_w_ukv=mla_w_ukv,
                  mla_q_norm=mla_q_norm, mla_k_norm=mla_k_norm, b_gate=b_gate, w_br_ssd=w_br_ssd,
                  w_br_swa=w_br_swa, w_br_mla=w_br_mla, w_out=w_out, peer_w_q=peer_w_q,
                  peer_keys=peer_keys, peer_u=peer_u, peer_v=peer_v)
    xl, xc = x, ctx
    for i in range(DEPTH):
        p = {k: v[i] for k, v in params.items()}
        xl, xc = layer(xl, xc, c, c_ctx, p, rope_swa, rope_mla, i < DEPTH - 1)
    return xl
```

```python
import functools

import jax
import jax.numpy as jnp
from jax import lax
from jax.experimental import pallas as pl
from jax.experimental.pallas import tpu as pltpu

D_MODEL = 1024
DEPTH = 2
GRID_W = 64
EPS = 1e-6
NEG_INF = -1e30
ROPE_BASE = 10000.0

SSD_D_INNER = 1024
SSD_HEAD_DIM = 64
SSD_HEADS = 16
SSD_GROUPS = 2
SSD_HEADS_PER_GROUP = 8
SSD_STATE = 64
SSD_CONV = 5
SSD_CHUNK = 128
SSD_CONV_CH = 1280

SWA_HEADS = 8
SWA_KV_HEADS = 2
SWA_HEAD_DIM = 64
SWA_WINDOW = 128
SWA_BLOCK = 128
SWA_SCALE = SWA_HEAD_DIM ** -0.5

MLA_HEADS = 8
MLA_Q_RANK = 384
MLA_KV_RANK = 256
MLA_NOPE = 64
MLA_ROPE = 32
MLA_V = 64
MLA_QK = MLA_NOPE + MLA_ROPE
MLA_BLOCK = 128
MLA_SCALE = MLA_QK ** -0.5
MLA_TQ = 512
MLA_TK = 768

PEER_HEADS = 8
PEER_N_KEYS = 128
PEER_D_KEY = 256
PEER_HALF = 128
PEER_TOPK = 16
PEER_BLOCK = 64

IN_SIZES = (SSD_D_INNER, SSD_CONV_CH, SSD_HEADS, SSD_HEADS,
            SWA_HEADS * SWA_HEAD_DIM, SWA_KV_HEADS * SWA_HEAD_DIM, SWA_KV_HEADS * SWA_HEAD_DIM,
            MLA_Q_RANK, MLA_KV_RANK, MLA_ROPE,
            D_MODEL, D_MODEL, D_MODEL)
IN_WIDTH = sum(IN_SIZES)

LANES = 128


def _round_up(n, m):
    return (n + m - 1) // m * m


def _mm_kernel(a_ref, b_ref, o_ref):
    o_ref[...] = jnp.dot(a_ref[...].astype(jnp.bfloat16), b_ref[...],
                         preferred_element_type=jnp.float32)


def _pick_tile(n, cap, unit):
    t = min(n, cap)
    while n % t or t % unit:
        t -= unit
    return t


def pmatmul(a, w):
    m, k = a.shape
    n = w.shape[1]
    n_pad = _round_up(n, LANES)
    wb = w.astype(jnp.bfloat16)
    if n_pad != n:
        wb = jnp.pad(wb, ((0, 0), (0, n_pad - n)))
    tm = _pick_tile(m, 512, 8)
    tn = _pick_tile(n_pad, 1024, LANES)
    out = pl.pallas_call(
        _mm_kernel,
        grid=(m // tm, n_pad // tn),
        in_specs=[pl.BlockSpec((tm, k), lambda i, j: (i, 0)),
                  pl.BlockSpec((k, tn), lambda i, j: (0, j))],
        out_specs=pl.BlockSpec((tm, tn), lambda i, j: (i, j)),
        out_shape=jax.ShapeDtypeStruct((m, n_pad), jnp.float32),
        compiler_params=pltpu.CompilerParams(dimension_semantics=("parallel", "parallel")),
        name="matmul",
    )(a, wb)
    return out[:, :n] if n_pad != n else out


def mm(a, w):
    lead = a.shape[:-1]
    return pmatmul(a.reshape(-1, a.shape[-1]), w).reshape(lead + (w.shape[1],))


def _flash_kernel(q_ref, k_ref, v_ref, o_ref, *, tk):
    f32 = jnp.float32
    q = q_ref[0, 0]
    tq = q.shape[0]
    dv = v_ref.shape[-1]
    n_chunks = k_ref.shape[2] // tk

    def body(c, carry):
        m, l, acc = carry
        off = pl.multiple_of(c * tk, tk)
        k = k_ref[0, 0, pl.ds(off, tk), :]
        v = v_ref[0, 0, pl.ds(off, tk), :]
        s = lax.dot_general(q, k, (((1,), (1,)), ((), ())), preferred_element_type=f32)
        m_new = jnp.maximum(m, jnp.max(s, axis=-1, keepdims=True))
        p = jnp.exp(s - m_new)
        alpha = jnp.exp(m - m_new)
        l = alpha * l + jnp.sum(p, axis=-1, keepdims=True)
        acc = alpha * acc + jnp.dot(p.astype(jnp.bfloat16), v, preferred_element_type=f32)
        return m_new, l, acc

    init = (jnp.full((tq, 1), -jnp.inf, f32), jnp.zeros((tq, 1), f32), jnp.zeros((tq, dv), f32))
    _, l, acc = lax.fori_loop(0, n_chunks, body, init, unroll=True)
    o_ref[0, 0] = acc / l


def flash_attention(q, k, v, tq, tk):
    bsz, nh, lq, dq = q.shape
    nk, dv = v.shape[2], v.shape[3]
    bf16 = jnp.bfloat16
    return pl.pallas_call(
        functools.partial(_flash_kernel, tk=tk),
        grid=(bsz, nh, lq // tq),
        in_specs=[pl.BlockSpec((1, 1, tq, dq), lambda b, h, i: (b, h, i, 0)),
                  pl.BlockSpec((1, 1, nk, dq), lambda b, h, i: (b, h, 0, 0)),
                  pl.BlockSpec((1, 1, nk, dv), lambda b, h, i: (b, h, 0, 0))],
        out_specs=pl.BlockSpec((1, 1, tq, dv), lambda b, h, i: (b, h, i, 0)),
        out_shape=jax.ShapeDtypeStruct((bsz, nh, lq, dv), jnp.float32),
        compiler_params=pltpu.CompilerParams(dimension_semantics=("parallel", "parallel", "parallel"),
                                             vmem_limit_bytes=PEER_VMEM_LIMIT),
        name="flash_attention",
    )(q.astype(bf16), k.astype(bf16), v.astype(bf16))


SWA_GROUP = SWA_HEADS // SWA_KV_HEADS
SWA_BAND = 3 * SWA_BLOCK


def _swa_kernel(q_ref, k_ref, v_ref, kc_ref, vc_ref, sink_ref, o_ref):
    f32 = jnp.float32
    bf16 = jnp.bfloat16
    n = pl.program_id(2)
    seq = k_ref.shape[2]
    rows = SWA_GROUP * SWA_BLOCK
    q = q_ref[0].reshape(rows, SWA_HEAD_DIM)
    start = pl.multiple_of(jnp.clip((n - 1) * SWA_BLOCK, 0, seq - SWA_BAND), SWA_BLOCK)
    kb = k_ref[0, 0, pl.ds(start, SWA_BAND), :]
    vb = v_ref[0, 0, pl.ds(start, SWA_BAND), :]
    nt = (((1,), (1,)), ((), ()))
    s_band = lax.dot_general(q, kb, nt, preferred_element_type=f32)
    s_ctx = lax.dot_general(q, kc_ref[0, 0], nt, preferred_element_type=f32)
    qpos = n * SWA_BLOCK + lax.broadcasted_iota(jnp.int32, (rows, SWA_BAND), 0) % SWA_BLOCK
    kpos = start + lax.broadcasted_iota(jnp.int32, (rows, SWA_BAND), 1)
    s_band = jnp.where(jnp.abs(kpos - qpos) <= SWA_WINDOW, s_band, NEG_INF)
    sink = sink_ref[0]
    m = jnp.maximum(jnp.maximum(jnp.max(s_band, axis=-1, keepdims=True),
                                jnp.max(s_ctx, axis=-1, keepdims=True)), sink)
    p_band = jnp.exp(s_band - m)
    p_ctx = jnp.exp(s_ctx - m)
    denom = (jnp.sum(p_band, axis=-1, keepdims=True) + jnp.sum(p_ctx, axis=-1, keepdims=True)
             + jnp.exp(sink - m))
    o = (jnp.dot(p_band.astype(bf16), vb, preferred_element_type=f32)
         + jnp.dot(p_ctx.astype(bf16), vc_ref[0, 0], preferred_element_type=f32))
    o_ref[0] = (o / denom).reshape(SWA_GROUP, SWA_BLOCK, SWA_HEAD_DIM)


def swa_attention(q, k, v, kc, vc, sink):
    bsz, _, seq, hd = q.shape
    n_ctx = kc.shape[2]
    bf16 = jnp.bfloat16
    sink_rows = jnp.repeat(sink.astype(jnp.float32), SWA_BLOCK).reshape(SWA_KV_HEADS, SWA_GROUP * SWA_BLOCK, 1)
    kv_spec = pl.BlockSpec((1, 1, seq, hd), lambda b, h, n: (b, h, 0, 0))
    ctx_spec = pl.BlockSpec((1, 1, n_ctx, hd), lambda b, h, n: (b, h, 0, 0))
    return pl.pallas_call(
        _swa_kernel,
        grid=(bsz, SWA_KV_HEADS, seq // SWA_BLOCK),
        in_specs=[pl.BlockSpec((1, SWA_GROUP, SWA_BLOCK, hd), lambda b, h, n: (b, h, n, 0)),
                  kv_spec, kv_spec, ctx_spec, ctx_spec,
                  pl.BlockSpec((1, SWA_GROUP * SWA_BLOCK, 1), lambda b, h, n: (h, 0, 0))],
        out_specs=pl.BlockSpec((1, SWA_GROUP, SWA_BLOCK, hd), lambda b, h, n: (b, h, n, 0)),
        out_shape=jax.ShapeDtypeStruct((bsz, SWA_HEADS, seq, hd), jnp.float32),
        compiler_params=pltpu.CompilerParams(dimension_semantics=("parallel", "parallel", "parallel")),
        name="swa_attention",
    )(q.astype(bf16), k.astype(bf16), v.astype(bf16), kc.astype(bf16), vc.astype(bf16), sink_rows)


SSD_PAIR = 2 * SSD_HEAD_DIM


def _softplus(x):
    return jnp.maximum(x, 0.0) + jnp.log1p(jnp.exp(-jnp.abs(x)))


def _ssd_kernel(u_ref, dt_ref, dtt_ref, bias_r_ref, bias_c_ref, alog_r_ref, alog_c_ref, dx_ref,
                y_ref, state_ref, *, reverse):
    f32 = jnp.float32
    bf16 = jnp.bfloat16
    q = SSD_CHUNK
    hi = lax.Precision.HIGHEST

    @pl.when(pl.program_id(1) == 0)
    def _():
        state_ref[...] = jnp.zeros_like(state_ref)

    row = lax.broadcasted_iota(jnp.int32, (q, q), 0)
    col = lax.broadcasted_iota(jnp.int32, (q, q), 1)
    before = (col >= row) if reverse else (col <= row)
    tri = before.astype(f32)

    dt = _softplus(dt_ref[0] + bias_r_ref[...])
    dtt = _softplus(dtt_ref[0] + bias_c_ref[...])
    a_r = -jnp.exp(alog_r_ref[...])
    a_c = -jnp.exp(alog_c_ref[...])
    cum = jnp.dot(tri, dt * a_r, precision=hi, preferred_element_type=f32)
    cumt = lax.dot_general(dtt * a_c, tri, (((1,), (1,)), ((), ())), precision=hi,
                           preferred_element_type=f32)
    tot = cum[0:1, :] if reverse else cum[q - 1:q, :]
    ecum = jnp.exp(cum)
    wgt = jnp.exp(tot - cum) * dt
    etot = jnp.exp(tot)

    lane_lo = lax.broadcasted_iota(jnp.int32, (q, SSD_PAIR), 1) < SSD_HEAD_DIM
    lane_lo_row = lane_lo[0:1, :]
    heads_per_group = SSD_HEADS_PER_GROUP
    for g in range(SSD_GROUPS):
        b_off = SSD_D_INNER + g * SSD_STATE
        c_off = SSD_D_INNER + SSD_GROUPS * SSD_STATE + g * SSD_STATE
        bg = u_ref[0, :, b_off:b_off + SSD_STATE].astype(bf16)
        cg = u_ref[0, :, c_off:c_off + SSD_STATE].astype(bf16)
        cb = lax.dot_general(cg, bg, (((1,), (1,)), ((), ())), preferred_element_type=f32)
        for kk in range(heads_per_group // 2):
            k = g * (heads_per_group // 2) + kk
            lanes = slice(k * SSD_PAIR, (k + 1) * SSD_PAIR)
            xp = u_ref[0, :, lanes]
            y = dx_ref[:, lanes] * xp
            for half in range(2):
                hh = 2 * k + half
                seg = cum[:, hh:hh + 1] - cumt[hh:hh + 1, :]
                decay = jnp.where(before, jnp.exp(seg), 0.0) * dtt[hh:hh + 1, :]
                m = (cb * decay).astype(bf16)
                xm = jnp.where(lane_lo if half == 0 else jnp.logical_not(lane_lo), xp, 0.0).astype(bf16)
                y = y + jnp.dot(m, xm, preferred_element_type=f32)
            e_pair = jnp.where(lane_lo, ecum[:, 2 * k:2 * k + 1], ecum[:, 2 * k + 1:2 * k + 2])
            w_pair = jnp.where(lane_lo, wgt[:, 2 * k:2 * k + 1], wgt[:, 2 * k + 1:2 * k + 2])
            t_pair = jnp.where(lane_lo_row, etot[:, 2 * k:2 * k + 1], etot[:, 2 * k + 1:2 * k + 2])
            prev = state_ref[:, lanes]
            y = y + jnp.dot(cg, prev.astype(bf16), preferred_element_type=f32) * e_pair
            y_ref[0, :, lanes] = y
            xw = (xp * w_pair).astype(bf16)
            s_new = lax.dot_general(bg, xw, (((0,), (0,)), ((), ())), preferred_element_type=f32)
            state_ref[:, lanes] = prev * t_pair + s_new


def ssd_scan_dir(u, dt_raw, bias, a_log, d_skip, reverse, n_ctx):
    bsz, t_all, _ = u.shape
    q = SSD_CHUNK
    nc = t_all // q
    ncc = n_ctx // q
    if reverse:
        def chunk(c):
            return jnp.where(c < ncc, ncc - 1 - c, nc + ncc - 1 - c)
    else:
        def chunk(c):
            return c
    f32 = jnp.float32
    row = lambda v: v.astype(f32).reshape(1, SSD_HEADS)
    colv = lambda v: v.astype(f32).reshape(SSD_HEADS, 1)
    small_r = pl.BlockSpec((1, SSD_HEADS), lambda b, c: (0, 0))
    small_c = pl.BlockSpec((SSD_HEADS, 1), lambda b, c: (0, 0))
    return pl.pallas_call(
        functools.partial(_ssd_kernel, reverse=reverse),
        grid=(bsz, nc),
        in_specs=[pl.BlockSpec((1, q, SSD_CONV_CH), lambda b, c: (b, chunk(c), 0)),
                  pl.BlockSpec((1, q, SSD_HEADS), lambda b, c: (b, chunk(c), 0)),
                  pl.BlockSpec((1, SSD_HEADS, q), lambda b, c: (b, 0, chunk(c))),
                  small_r, small_c, small_r, small_c,
                  pl.BlockSpec((1, SSD_D_INNER), lambda b, c: (0, 0))],
        out_specs=pl.BlockSpec((1, q, SSD_D_INNER), lambda b, c: (b, chunk(c), 0)),
        out_shape=jax.ShapeDtypeStruct((bsz, t_all, SSD_D_INNER), f32),
        scratch_shapes=[pltpu.VMEM((SSD_STATE, SSD_D_INNER), f32)],
        compiler_params=pltpu.CompilerParams(dimension_semantics=("parallel", "arbitrary")),
        name="ssd_scan_bwd" if reverse else "ssd_scan_fwd",
    )(u, dt_raw, jnp.swapaxes(dt_raw, 1, 2), row(bias), colv(bias), row(a_log), colv(a_log),
      jnp.repeat(d_skip.astype(f32), SSD_HEAD_DIM).reshape(1, SSD_D_INNER))


PROJ_TM = 256
QKV_W = (SWA_HEADS + 2 * SWA_KV_HEADS) * SWA_HEAD_DIM
CQ_W = MLA_Q_RANK + MLA_KV_RANK
MISC_W = LANES


def _rms(x):
    return x * lax.rsqrt(jnp.mean(x * x, axis=-1, keepdims=True) + EPS)


def _premix_kernel(x_ref, g_ref, sc_ref, sh_ref, bg_ref, wz_ref, wx_ref, wq_ref, wc_ref, wg_ref, wm_ref,
                   z_ref, xbc_ref, qkv_ref, cq_ref, gate_ref, misc_ref):
    f32 = jnp.float32
    h = (_rms(x_ref[...]) * g_ref[...] * (1.0 + sc_ref[0]) + sh_ref[0]).astype(jnp.bfloat16)
    z_ref[...] = jnp.dot(h, wz_ref[...], preferred_element_type=f32)
    xbc_ref[...] = jnp.dot(h, wx_ref[...], preferred_element_type=f32)
    qkv_ref[...] = jnp.dot(h, wq_ref[...], preferred_element_type=f32)
    cq_ref[...] = jnp.dot(h, wc_ref[...], preferred_element_type=f32)
    gate_ref[...] = jax.nn.sigmoid(jnp.dot(h, wg_ref[...], preferred_element_type=f32) + bg_ref[...])
    misc_ref[...] = jnp.dot(h, wm_ref[...], preferred_element_type=f32)


def _pack_w_in(w_in):
    pts = [0] + split_points() + [IN_WIDTH]
    col = lambda i: w_in[:, pts[i]:pts[i + 1]]
    bf16 = jnp.bfloat16
    pad = jnp.zeros((w_in.shape[0], MISC_W - MLA_ROPE - 2 * SSD_HEADS), w_in.dtype)
    return (col(0).astype(bf16), col(1).astype(bf16),
            jnp.concatenate([col(4), col(5), col(6)], axis=1).astype(bf16),
            jnp.concatenate([col(7), col(8)], axis=1).astype(bf16),
            jnp.concatenate([col(10), col(11), col(12)], axis=1).astype(bf16),
            jnp.concatenate([col(9), col(2), col(3), pad], axis=1).astype(bf16))


def _batch_row_spec(width, tiles_per_batch):
    return pl.BlockSpec((1, 1, width), lambda t: (t // tiles_per_batch, 0, 0))


def premix(x, norm_g, scale, shift, b_gate, w_groups):
    bsz, t_len, d = x.shape
    n = bsz * t_len
    tm = min(PROJ_TM, t_len)
    f32 = jnp.float32
    widths = (SSD_D_INNER, SSD_CONV_CH, QKV_W, CQ_W, 3 * D_MODEL, MISC_W)
    const = lambda shape: pl.BlockSpec(shape, lambda t: (0,) * len(shape))
    tok = lambda w: pl.BlockSpec((tm, w), lambda t: (t, 0))
    outs = pl.pallas_call(
        _premix_kernel,
        grid=(n // tm,),
        in_specs=[tok(d), const((1, d)), _batch_row_spec(d, t_len // tm), _batch_row_spec(d, t_len // tm),
                  const((1, 3 * D_MODEL))] + [const((d, w)) for w in widths],
        out_specs=[tok(w) for w in widths],
        out_shape=[jax.ShapeDtypeStruct((n, w), f32) for w in widths],
        compiler_params=pltpu.CompilerParams(dimension_semantics=("parallel",),
                                             vmem_limit_bytes=PEER_VMEM_LIMIT),
        name="premix",
    )(x.reshape(n, d), norm_g.reshape(1, d), scale, shift, b_gate.reshape(1, 3 * D_MODEL), *w_groups)
    return [o.reshape(bsz, t_len, -1) for o in outs]


def _merge_kernel(x_ref, yf_ref, yb_ref, z_ref, yw_ref, ym_ref, gate_ref, gs_ref, g1_ref, n2_ref, sc_ref,
                  sh_ref, w1_ref, w2_ref, w3_ref, wo_ref, xo_ref, h2_ref):
    f32 = jnp.float32
    bf16 = jnp.bfloat16
    d = D_MODEL
    z = z_ref[...]
    ys = _rms((yf_ref[...] + yb_ref[...]) * (z * jax.nn.sigmoid(z))) * gs_ref[...]
    m = (gate_ref[:, 0:d] * jnp.dot(ys.astype(bf16), w1_ref[...], preferred_element_type=f32)
         + gate_ref[:, d:2 * d] * jnp.dot(yw_ref[...].astype(bf16), w2_ref[...], preferred_element_type=f32)
         + gate_ref[:, 2 * d:3 * d] * jnp.dot(ym_ref[...].astype(bf16), w3_ref[...], preferred_element_type=f32))
    mix = jnp.dot(m.astype(bf16), wo_ref[...], preferred_element_type=f32)
    xn = x_ref[...] + g1_ref[0] * mix
    xo_ref[...] = xn
    h2_ref[...] = (_rms(xn) * n2_ref[...] * (1.0 + sc_ref[0]) + sh_ref[0]).astype(bf16)


def merge_norm(x, yf, yb, y_block, z, yw, ym, gates, p, g1, sc2, sh2):
    bsz, t_len, d = x.shape
    n = bsz * t_len
    tm = min(PROJ_TM, t_len)
    bf16 = jnp.bfloat16
    const = lambda shape: pl.BlockSpec(shape, lambda t: (0,) * len(shape))
    tok = lambda w: pl.BlockSpec((tm, w), lambda t: (t, 0))
    brow = _batch_row_spec(d, t_len // tm)
    ytok = pl.BlockSpec((tm, d), lambda t: (y_block(t, tm), 0))
    flat = lambda a: a.reshape(-1, a.shape[-1])
    swa_w = SWA_HEADS * SWA_HEAD_DIM
    mla_w = MLA_HEADS * MLA_V
    xo, h2 = pl.pallas_call(
        _merge_kernel,
        grid=(n // tm,),
        in_specs=[tok(d), ytok, ytok, tok(d), tok(swa_w), tok(mla_w), tok(3 * d),
                  const((1, d)), brow, const((1, d)), brow, brow,
                  const((d, d)), const((swa_w, d)), const((mla_w, d)), const((d, d))],
        out_specs=[tok(d), tok(d)],
        out_shape=[jax.ShapeDtypeStruct((n, d), jnp.float32), jax.ShapeDtypeStruct((n, d), bf16)],
        compiler_params=pltpu.CompilerParams(dimension_semantics=("parallel",),
                                             vmem_limit_bytes=PEER_VMEM_LIMIT),
        name="merge_norm",
    )(flat(x), flat(yf), flat(yb), flat(z), flat(yw), flat(ym), flat(gates),
      p['ssd_norm_g'].reshape(1, d), g1, p['norm2_g'].reshape(1, d), sc2, sh2,
      p['w_br_ssd'].astype(bf16), p['w_br_swa'].astype(bf16), p['w_br_mla'].astype(bf16),
      p['w_out'].astype(bf16))
    return xo.reshape(bsz, t_len, d), h2.reshape(bsz, t_len, d)


def split_points():
    pts, acc = [], 0
    for s in IN_SIZES[:-1]:
        acc += s
        pts.append(acc)
    return pts


def rms_norm(x, g):
    xf = x.astype(jnp.float32)
    y = xf * lax.rsqrt(jnp.mean(xf * xf, axis=-1, keepdims=True) + EPS)
    return (y * g.astype(jnp.float32)).astype(x.dtype)


def axial_rope(rows, dim):
    pairs = dim // 4
    freqs = ROPE_BASE ** (-jnp.arange(pairs, dtype=jnp.float32) / pairs)
    pos_r = jnp.repeat(jnp.arange(rows, dtype=jnp.float32), GRID_W)
    pos_c = jnp.tile(jnp.arange(GRID_W, dtype=jnp.float32), rows)
    ang = jnp.concatenate([pos_r[:, None] * freqs, pos_c[:, None] * freqs], axis=-1)
    return jnp.cos(ang), jnp.sin(ang)


def apply_rope(x, cos, sin):
    d2 = x.shape[-1] // 2
    x1, x2 = x[..., :d2], x[..., d2:]
    cs, sn = cos[:, None, :], sin[:, None, :]
    return jnp.concatenate([x1 * cs - x2 * sn, x2 * cs + x1 * sn], axis=-1).astype(x.dtype)


def dwconv_centred(u, w, b):
    pad = (SSD_CONV - 1) // 2
    y = lax.conv_general_dilated(u, w[:, None, :].astype(u.dtype), window_strides=(1,),
                                 padding=[(pad, pad)], dimension_numbers=('NWC', 'WIO', 'NWC'),
                                 feature_group_count=u.shape[-1])
    return y + b


def ssd_mixer(lat, ctx, p, with_ctx_out):
    f32 = jnp.float32

    def conv_act(xbc):
        return jax.nn.silu(dwconv_centred(xbc, p['ssd_conv_w'], p['ssd_conv_b']))

    n_ctx = ctx[1].shape[1]
    u_all = jnp.concatenate([conv_act(ctx[1]), conv_act(lat[1])], axis=1)
    ys = []
    for direction in range(2):
        dt_raw = jnp.concatenate([ctx[2 + direction], lat[2 + direction]], axis=1)
        ys.append(ssd_scan_dir(u_all, dt_raw, p['ssd_dt_bias'][direction], p['ssd_a_log'][direction],
                               p['ssd_d'][direction], direction == 1, n_ctx))
    return ys


def swa_mixer(lat, ctx, p, rope, with_ctx_out):
    f32 = jnp.float32
    grp = SWA_HEADS // SWA_KV_HEADS

    def heads(q, k, v):
        bsz, T, _ = q.shape
        q = rms_norm(q.reshape(bsz, T, SWA_HEADS, SWA_HEAD_DIM), p['swa_q_norm'])
        k = rms_norm(k.reshape(bsz, T, SWA_KV_HEADS, SWA_HEAD_DIM), p['swa_k_norm'])
        return q, k, v.reshape(bsz, T, SWA_KV_HEADS, SWA_HEAD_DIM)

    ql, kl, vl = heads(*lat)
    qc, kc, vc = heads(*ctx)
    cos, sin = rope
    ql = apply_rope(ql, cos, sin) * SWA_SCALE
    kl = apply_rope(kl, cos, sin)
    bsz, L = ql.shape[:2]
    sink = p['swa_sink'].astype(f32).reshape(SWA_KV_HEADS, grp)
    hm = lambda t: jnp.swapaxes(t, 1, 2)
    o = swa_attention(hm(ql), hm(kl), hm(vl), hm(kc), hm(vc), p['swa_sink'])
    y_l = hm(o).reshape(bsz, L, SWA_HEADS * SWA_HEAD_DIM)

    y_c = None
    if with_ctx_out:
        qcb = (qc * SWA_SCALE).reshape(bsz, -1, SWA_KV_HEADS, grp, SWA_HEAD_DIM)
        s = jnp.einsum('bqkgd,bjkd->bkgqj', qcb, kc).astype(f32)
        sink_c = jnp.broadcast_to(sink[None, :, :, None, None], s.shape[:-1] + (1,))
        pc = jax.nn.softmax(jnp.concatenate([s, sink_c], axis=-1), axis=-1)[..., :-1].astype(vc.dtype)
        y_c = jnp.einsum('bkgqj,bjkd->bqkgd', pc, vc).reshape(bsz, -1, SWA_HEADS * SWA_HEAD_DIM)
    return y_l, y_c


def mla_mixer(lat, ctx, p, rope, with_ctx_out):
    f32 = jnp.float32

    def heads(cq, ckv, kr):
        bsz, T, _ = cq.shape
        q = mm(rms_norm(cq, p['mla_q_a_norm']), p['mla_w_uq']).reshape(bsz, T, MLA_HEADS, MLA_QK)
        kv = mm(rms_norm(ckv, p['mla_kv_a_norm']), p['mla_w_ukv']).reshape(bsz, T, MLA_HEADS, MLA_NOPE + MLA_V)
        k = jnp.concatenate([kv[..., :MLA_NOPE],
                             jnp.broadcast_to(kr[:, :, None, :], (bsz, T, MLA_HEADS, MLA_ROPE))], axis=-1)
        return rms_norm(q, p['mla_q_norm']), rms_norm(k, p['mla_k_norm']), kv[..., MLA_NOPE:]

    cos, sin = rope

    def rope_tail(t):
        return jnp.concatenate([t[..., :MLA_NOPE], apply_rope(t[..., MLA_NOPE:], cos, sin)], axis=-1)

    ql, kl, vl = heads(*lat)
    qc, kc, vc = heads(*ctx)
    ql = rope_tail(ql) * MLA_SCALE
    kl = rope_tail(kl)
    bsz, L = ql.shape[:2]
    nb = L // MLA_BLOCK
    k_all = jnp.concatenate([kc, kl], axis=1)
    v_all = jnp.concatenate([vc, vl], axis=1)

    def attend(qb, k, v, tq, tk):
        hm = lambda t: jnp.swapaxes(t, 1, 2)
        o = flash_attention(hm(qb), hm(k), hm(v), tq, tk)
        return hm(o).reshape(qb.shape[0], qb.shape[1], MLA_HEADS * MLA_V)

    y_l = attend(ql, k_all, v_all, MLA_TQ, MLA_TK)
    y_c = None
    if with_ctx_out:
        n_ctx = qc.shape[1]
        y_c = attend(qc * MLA_SCALE, kc, vc, n_ctx, n_ctx)
    return y_l, y_c


PEER_ROUTE_TM = 256
PEER_TM = 512
PEER_EC = 512
PEER_VMEM_LIMIT = 56 * 1024 * 1024
_SQRT_HALF = 0.7071067811865476


def _top16_rows(s):
    rows = []
    for _ in range(PEER_TOPK):
        m = jnp.max(s, axis=0, keepdims=True)
        rows.append(m)
        s = jnp.where(s == m, -jnp.inf, s)
    return jnp.concatenate(rows, axis=0)


def _peer_route_kernel(h_ref, wq_ref, keys_ref, s0_ref, s1_ref, a_ref, b_ref, tau_ref, q_scr):
    f32 = jnp.float32
    q = jnp.dot(h_ref[...].astype(jnp.bfloat16), wq_ref[...], preferred_element_type=f32)
    q_scr[...] = q.astype(jnp.bfloat16)

    def head_body(hd, carry):
        def scores(p):
            col = pl.multiple_of((hd * 2 + p) * PEER_HALF, PEER_HALF)
            qs = q_scr[:, pl.ds(col, PEER_HALF)]
            return lax.dot_general(keys_ref[p * PEER_HEADS + hd], qs, (((1,), (1,)), ((), ())),
                                   preferred_element_type=f32)

        s0 = scores(0)
        s1 = scores(1)
        a_top = _top16_rows(s0)
        b_top = _top16_rows(s1)
        cand = [a_top[0:1] + b_top]
        cand += [a_top[r:r + 1] + b_top[0:8] for r in range(1, 8)]
        cand.append(a_top[8:16] + b_top[0:1])
        best = _top16_rows(jnp.concatenate(cand, axis=0))
        z = jnp.sum(jnp.exp(best - best[0:1]), axis=0, keepdims=True)
        a = jnp.exp(s0 - a_top[0:1])
        b = jnp.exp(s1 - b_top[0:1]) / z
        tau = best[PEER_TOPK - 1:PEER_TOPK]
        for tl in range(s0.shape[1] // LANES):
            ts = slice(tl * LANES, (tl + 1) * LANES)
            s0_ref[hd, tl] = s0[:, ts]
            s1_ref[hd, tl] = s1[:, ts]
            a_ref[hd, tl] = a[:, ts]
            b_ref[hd, tl] = b[:, ts]
            tau_ref[hd, tl] = tau[:, ts]
        return carry

    lax.fori_loop(0, PEER_HEADS, head_body, 0)


def _gelu_exact(x):
    return 0.5 * x * (1.0 + lax.erf(x * _SQRT_HALF))


def _peer_expert_kernel(h_ref, u_ref, vt_ref, s0_ref, s1_ref, a_ref, b_ref, tau_ref, x_ref, g2_ref,
                        o_ref, acc_ref, act_scr, g_scr):
    f32 = jnp.float32
    j = pl.program_id(1)
    rows_per_step = PEER_EC // PEER_N_KEYS

    @pl.when(j == 0)
    def _():
        acc_ref[...] = jnp.zeros_like(acc_ref)

    act_scr[...] = lax.dot_general(u_ref[...], h_ref[...], (((1,), (1,)), ((), ())),
                                   preferred_element_type=f32)
    n_tl = PEER_TM // LANES

    def tile_body(t, carry):
        ii = t // n_tl
        tl = t % n_tl
        i = j * rows_per_step + ii
        w = jnp.zeros((PEER_N_KEYS, LANES), f32)
        for hd in range(PEER_HEADS):
            s0_row = s0_ref[hd, tl, pl.ds(i, 1), :]
            a_row = a_ref[hd, tl, pl.ds(i, 1), :]
            keep = (s1_ref[hd, tl] + s0_row) >= tau_ref[hd, tl]
            w = w + jnp.where(keep, b_ref[hd, tl], 0.0) * a_row
        es = pl.ds(pl.multiple_of(ii * PEER_N_KEYS, PEER_N_KEYS), PEER_N_KEYS)
        ts = pl.ds(pl.multiple_of(tl * LANES, LANES), LANES)
        g_scr[es, ts] = (w * _gelu_exact(act_scr[es, ts])).astype(jnp.bfloat16)
        return carry

    lax.fori_loop(0, rows_per_step * n_tl, tile_body, 0)
    acc_ref[...] += jnp.dot(vt_ref[...], g_scr[...], preferred_element_type=f32)

    @pl.when(j == pl.num_programs(1) - 1)
    def _():
        o_ref[...] = x_ref[...] + g2_ref[0] * acc_ref[...].T


def peer_ffn(h, x, g2_tiles, w_q, sub_keys, u_tab, v_tab):
    bsz, T, D = h.shape
    n_tok = bsz * T
    n_exp = u_tab.shape[0]
    hf = h.reshape(n_tok, D)
    f32 = jnp.float32
    bf16 = jnp.bfloat16
    keys = sub_keys.reshape(2 * PEER_HEADS, PEER_N_KEYS, PEER_HALF).astype(bf16)
    tm = PEER_ROUTE_TM
    n_tiles = n_tok // LANES
    big = jax.ShapeDtypeStruct((PEER_HEADS, n_tiles, PEER_N_KEYS, LANES), f32)
    small = jax.ShapeDtypeStruct((PEER_HEADS, n_tiles, 1, LANES), f32)
    big_spec = pl.BlockSpec((PEER_HEADS, tm // LANES, PEER_N_KEYS, LANES), lambda t: (0, t, 0, 0))
    small_spec = pl.BlockSpec((PEER_HEADS, tm // LANES, 1, LANES), lambda t: (0, t, 0, 0))
    s0, s1, a, b, tau = pl.pallas_call(
        _peer_route_kernel,
        grid=(n_tok // tm,),
        in_specs=[pl.BlockSpec((tm, D), lambda t: (t, 0)),
                  pl.BlockSpec((D, PEER_HEADS * PEER_D_KEY), lambda t: (0, 0)),
                  pl.BlockSpec((2 * PEER_HEADS, PEER_N_KEYS, PEER_HALF), lambda t: (0, 0, 0))],
        out_specs=[big_spec, big_spec, big_spec, big_spec, small_spec],
        out_shape=[big, big, big, big, small],
        scratch_shapes=[pltpu.VMEM((tm, PEER_HEADS * PEER_D_KEY), bf16)],
        compiler_params=pltpu.CompilerParams(dimension_semantics=("parallel",),
                                             vmem_limit_bytes=PEER_VMEM_LIMIT),
        name="peer_route",
    )(hf, w_q.astype(bf16), keys)

    tm = PEER_TM
    ec = PEER_EC
    big_spec = pl.BlockSpec((PEER_HEADS, tm // LANES, PEER_N_KEYS, LANES), lambda t, j: (0, t, 0, 0))
    small_spec = pl.BlockSpec((PEER_HEADS, tm // LANES, 1, LANES), lambda t, j: (0, t, 0, 0))
    out = pl.pallas_call(
        _peer_expert_kernel,
        grid=(n_tok // tm, n_exp // ec),
        in_specs=[pl.BlockSpec((tm, D), lambda t, j: (t, 0)),
                  pl.BlockSpec((ec, D), lambda t, j: (j, 0)),
                  pl.BlockSpec((D, ec), lambda t, j: (0, j)),
                  big_spec, big_spec, big_spec, big_spec, small_spec,
                  pl.BlockSpec((tm, D), lambda t, j: (t, 0)),
                  pl.BlockSpec((1, 1, D), lambda t, j: (t, 0, 0))],
        out_specs=pl.BlockSpec((tm, D), lambda t, j: (t, 0)),
        out_shape=jax.ShapeDtypeStruct((n_tok, D), f32),
        scratch_shapes=[pltpu.VMEM((D, tm), f32), pltpu.VMEM((ec, tm), f32), pltpu.VMEM((ec, tm), bf16)],
        compiler_params=pltpu.CompilerParams(dimension_semantics=("parallel", "arbitrary"),
                                             vmem_limit_bytes=PEER_VMEM_LIMIT),
        name="peer_expert",
    )(hf, u_tab.astype(bf16), v_tab.T.astype(bf16), s0, s1, a, b, tau, x.reshape(n_tok, D), g2_tiles)
    return out.reshape(bsz, T, D)


def _split_cols(t, sizes):
    out, acc = [], 0
    for s in sizes:
        out.append(t[..., acc:acc + s])
        acc += s
    return out


def layer(xl, xc, c, c_ctx, p, rope_swa, rope_mla, with_ctx_out):
    bsz, seq, d = xl.shape
    n_ctx = xc.shape[1]
    mod_l = (jax.nn.silu(c) @ p['w_mod'] + p['b_mod'])[:, None, :]
    mod_c = jnp.broadcast_to((jax.nn.silu(c_ctx) @ p['w_mod'] + p['b_mod'])[None, None, :], mod_l.shape)
    sh1_l, sc1_l, g1_l, sh2_l, sc2_l, g2_l = jnp.split(mod_l, 6, axis=-1)
    sh1_c, sc1_c, g1_c, sh2_c, sc2_c, g2_c = jnp.split(mod_c, 6, axis=-1)

    w_groups = _pack_w_in(p['w_in'])
    swa_sizes = (SWA_HEADS * SWA_HEAD_DIM, SWA_KV_HEADS * SWA_HEAD_DIM, SWA_KV_HEADS * SWA_HEAD_DIM)
    misc_sizes = (MLA_ROPE, SSD_HEADS, SSD_HEADS)

    def project(x, sc, sh):
        z, xbc, qkv, cq, gates, misc = premix(x, p['norm1_g'], sc, sh, p['b_gate'], w_groups)
        kr, dtf, dtb = _split_cols(misc, misc_sizes)
        return dict(z=z, gates=gates, ssd=(None, xbc, dtf, dtb), swa=_split_cols(qkv, swa_sizes),
                    mla=_split_cols(cq, (MLA_Q_RANK, MLA_KV_RANK)) + [kr])

    pl_ = project(xl, sc1_l, sh1_l)
    pc = project(xc, sc1_c, sh1_c)
    yf, yb = ssd_mixer(pl_['ssd'], pc['ssd'], p, with_ctx_out)
    y_swa_l, y_swa_c = swa_mixer(pl_['swa'], pc['swa'], p, rope_swa, with_ctx_out)
    y_mla_l, y_mla_c = mla_mixer(pl_['mla'], pc['mla'], p, rope_mla, with_ctx_out)

    def lat_block(t, tm):
        return (t // (seq // tm)) * ((n_ctx + seq) // tm) + n_ctx // tm + t % (seq // tm)

    def ctx_block(t, tm):
        return (t // (n_ctx // tm)) * ((n_ctx + seq) // tm) + t % (n_ctx // tm)

    peer = (p['peer_w_q'], p['peer_keys'], p['peer_u'], p['peer_v'])
    xl, h2_l = merge_norm(xl, yf, yb, lat_block, pl_['z'], y_swa_l, y_mla_l, pl_['gates'], p,
                          g1_l, sc2_l, sh2_l)
    if with_ctx_out:
        xc, h2_c = merge_norm(xc, yf, yb, ctx_block, pc['z'], y_swa_c, y_mla_c, pc['gates'], p,
                              g1_c, sc2_c, sh2_c)
        g2_c_tiles = jnp.broadcast_to(g2_c[:1], (bsz * n_ctx // PEER_TM, 1, d))
        xc = peer_ffn(h2_c, xc, g2_c_tiles, *peer)
    xl = peer_ffn(h2_l, xl, jnp.repeat(g2_l, seq // PEER_TM, axis=0), *peer)
    return xl, xc


def kernel(x, c, ctx, c_ctx, w_mod, b_mod, norm1_g, norm2_g, w_in, ssd_conv_w, ssd_conv_b, ssd_dt_bias,
           ssd_a_log, ssd_d, ssd_norm_g, swa_q_norm, swa_k_norm, swa_sink, mla_q_a_norm, mla_kv_a_norm,
           mla_w_uq, mla_w_ukv, mla_q_norm, mla_k_norm, b_gate, w_br_ssd, w_br_swa, w_br_mla, w_out,
           peer_w_q, peer_keys, peer_u, peer_v):
    L = x.shape[1]
    rows = L // GRID_W
    rope_swa = axial_rope(rows, SWA_HEAD_DIM)
    rope_mla = axial_rope(rows, MLA_ROPE)
    params = dict(w_mod=w_mod, b_mod=b_mod, norm1_g=norm1_g, norm2_g=norm2_g, w_in=w_in,
                  ssd_conv_w=ssd_conv_w, ssd_conv_b=ssd_conv_b, ssd_dt_bias=ssd_dt_bias,
                  ssd_a_log=ssd_a_log, ssd_d=ssd_d, ssd_norm_g=ssd_norm_g, swa_q_norm=swa_q_norm,
                  swa_k_norm=swa_k_norm, swa_sink=swa_sink, mla_q_a_norm=mla_q_a_norm,
                  mla_kv_a_norm=mla_kv_a_norm, mla_w_uq=mla_w_uq, mla_w_ukv=mla_w_ukv,
                  mla_q_norm=mla_q_norm, mla_k_norm=mla_k_norm, b_gate=b_gate, w_br_ssd=w_br_ssd,
                  w_br_swa=w_br_swa, w_br_mla=w_br_mla, w_out=w_out, peer_w_q=peer_w_q,
                  peer_keys=peer_keys, peer_u=peer_u, peer_v=peer_v)
    xl, xc = x, ctx
    for i in range(DEPTH):
        p = {k: v[i] for k, v in params.items()}
        xl, xc = layer(xl, xc, c, c_ctx, p, rope_swa, rope_mla, i < DEPTH - 1)
    return xl
```

```python
import functools

import jax
import jax.numpy as jnp
from jax import lax
from jax.experimental import pallas as pl
from jax.experimental.pallas import tpu as pltpu

D_MODEL = 1024
DEPTH = 2
GRID_W = 64
EPS = 1e-6
NEG_INF = -1e30
ROPE_BASE = 10000.0

SSD_D_INNER = 1024
SSD_HEAD_DIM = 64
SSD_HEADS = 16
SSD_GROUPS = 2
SSD_HEADS_PER_GROUP = 8
SSD_STATE = 64
SSD_CONV = 5
SSD_CHUNK = 128
SSD_CONV_CH = 1280

SWA_HEADS = 8
SWA_KV_HEADS = 2
SWA_HEAD_DIM = 64
SWA_WINDOW = 128
SWA_BLOCK = 128
SWA_SCALE = SWA_HEAD_DIM ** -0.5

MLA_HEADS = 8
MLA_Q_RANK = 384
MLA_KV_RANK = 256
MLA_NOPE = 64
MLA_ROPE = 32
MLA_V = 64
MLA_QK = MLA_NOPE + MLA_ROPE
MLA_BLOCK = 128
MLA_SCALE = MLA_QK ** -0.5
MLA_TQ = 512
MLA_TK = 768

PEER_HEADS = 8
PEER_N_KEYS = 128
PEER_D_KEY = 256
PEER_HALF = 128
PEER_TOPK = 16
PEER_BLOCK = 64

IN_SIZES = (SSD_D_INNER, SSD_CONV_CH, SSD_HEADS, SSD_HEADS,
            SWA_HEADS * SWA_HEAD_DIM, SWA_KV_HEADS * SWA_HEAD_DIM, SWA_KV_HEADS * SWA_HEAD_DIM,
            MLA_Q_RANK, MLA_KV_RANK, MLA_ROPE,
            D_MODEL, D_MODEL, D_MODEL)
IN_WIDTH = sum(IN_SIZES)

LANES = 128


def _round_up(n, m):
    return (n + m - 1) // m * m


def _mm_kernel(a_ref, b_ref, o_ref):
    o_ref[...] = jnp.dot(a_ref[...].astype(jnp.bfloat16), b_ref[...],
                         preferred_element_type=jnp.float32)


def _pick_tile(n, cap, unit):
    t = min(n, cap)
    while n % t or t % unit:
        t -= unit
    return t


def pmatmul(a, w):
    m, k = a.shape
    n = w.shape[1]
    n_pad = _round_up(n, LANES)
    wb = w.astype(jnp.bfloat16)
    if n_pad != n:
        wb = jnp.pad(wb, ((0, 0), (0, n_pad - n)))
    tm = _pick_tile(m, 512, 8)
    tn = _pick_tile(n_pad, 1024, LANES)
    out = pl.pallas_call(
        _mm_kernel,
        grid=(m // tm, n_pad // tn),
        in_specs=[pl.BlockSpec((tm, k), lambda i, j: (i, 0)),
                  pl.BlockSpec((k, tn), lambda i, j: (0, j))],
        out_specs=pl.BlockSpec((tm, tn), lambda i, j: (i, j)),
        out_shape=jax.ShapeDtypeStruct((m, n_pad), jnp.float32),
        compiler_params=pltpu.CompilerParams(dimension_semantics=("parallel", "parallel")),
        name="matmul",
    )(a, wb)
    return out[:, :n] if n_pad != n else out


def mm(a, w):
    lead = a.shape[:-1]
    return pmatmul(a.reshape(-1, a.shape[-1]), w).reshape(lead + (w.shape[1],))


def _flash_kernel(*refs, tk, has_sink):
    f32 = jnp.float32
    if has_sink:
        q_ref, k_ref, v_ref, sink_ref, o_ref = refs
    else:
        q_ref, k_ref, v_ref, o_ref = refs
    q = q_ref[0, 0]
    tq = q.shape[0]
    dv = v_ref.shape[-1]
    n_chunks = k_ref.shape[2] // tk

    def body(c, carry):
        m, l, acc = carry
        off = pl.multiple_of(c * tk, tk)
        k = k_ref[0, 0, pl.ds(off, tk), :]
        v = v_ref[0, 0, pl.ds(off, tk), :]
        s = lax.dot_general(q, k, (((1,), (1,)), ((), ())), preferred_element_type=f32)
        m_new = jnp.maximum(m, jnp.max(s, axis=-1, keepdims=True))
        p = jnp.exp(s - m_new)
        alpha = jnp.exp(m - m_new)
        l = alpha * l + jnp.sum(p, axis=-1, keepdims=True)
        acc = alpha * acc + jnp.dot(p.astype(jnp.bfloat16), v, preferred_element_type=f32)
        return m_new, l, acc

    if has_sink:
        init = (jnp.broadcast_to(sink_ref[0], (tq, 1)), jnp.ones((tq, 1), f32), jnp.zeros((tq, dv), f32))
    else:
        init = (jnp.full((tq, 1), -jnp.inf, f32), jnp.zeros((tq, 1), f32), jnp.zeros((tq, dv), f32))
    _, l, acc = lax.fori_loop(0, n_chunks, body, init, unroll=True)
    o_ref[0, 0] = acc / l


def flash_attention(q, k, v, tq, tk, sink=None):
    bsz, nh, lq, dq = q.shape
    nk, dv = v.shape[2], v.shape[3]
    grp = nh // k.shape[1]
    bf16 = jnp.bfloat16
    in_specs = [pl.BlockSpec((1, 1, tq, dq), lambda b, h, i: (b, h, i, 0)),
                pl.BlockSpec((1, 1, nk, dq), lambda b, h, i: (b, h // grp, 0, 0)),
                pl.BlockSpec((1, 1, nk, dv), lambda b, h, i: (b, h // grp, 0, 0))]
    args = [q.astype(bf16), k.astype(bf16), v.astype(bf16)]
    if sink is not None:
        in_specs.append(pl.BlockSpec((1, 1, 1), lambda b, h, i: (h, 0, 0)))
        args.append(sink.astype(jnp.float32).reshape(nh, 1, 1))
    return pl.pallas_call(
        functools.partial(_flash_kernel, tk=tk, has_sink=sink is not None),
        grid=(bsz, nh, lq // tq),
        in_specs=in_specs,
        out_specs=pl.BlockSpec((1, 1, tq, dv), lambda b, h, i: (b, h, i, 0)),
        out_shape=jax.ShapeDtypeStruct((bsz, nh, lq, dv), jnp.float32),
        compiler_params=pltpu.CompilerParams(dimension_semantics=("parallel", "parallel", "parallel"),
                                             vmem_limit_bytes=PEER_VMEM_LIMIT),
        name="flash_attention",
    )(*args)


SWA_GROUP = SWA_HEADS // SWA_KV_HEADS
SWA_BAND = 3 * SWA_BLOCK


def _swa_kernel(q_ref, k_ref, v_ref, kc_ref, vc_ref, sink_ref, o_ref):
    f32 = jnp.float32
    bf16 = jnp.bfloat16
    n = pl.program_id(2)
    seq = k_ref.shape[2]
    rows = SWA_GROUP * SWA_BLOCK
    q = q_ref[0].reshape(rows, SWA_HEAD_DIM)
    start = pl.multiple_of(jnp.clip((n - 1) * SWA_BLOCK, 0, seq - SWA_BAND), SWA_BLOCK)
    kb = k_ref[0, 0, pl.ds(start, SWA_BAND), :]
    vb = v_ref[0, 0, pl.ds(start, SWA_BAND), :]
    nt = (((1,), (1,)), ((), ()))
    s_band = lax.dot_general(q, kb, nt, preferred_element_type=f32)
    s_ctx = lax.dot_general(q, kc_ref[0, 0], nt, preferred_element_type=f32)
    qpos = n * SWA_BLOCK + lax.broadcasted_iota(jnp.int32, (rows, SWA_BAND), 0) % SWA_BLOCK
    kpos = start + lax.broadcasted_iota(jnp.int32, (rows, SWA_BAND), 1)
    s_band = jnp.where(jnp.abs(kpos - qpos) <= SWA_WINDOW, s_band, NEG_INF)
    sink = sink_ref[0]
    m = jnp.maximum(jnp.maximum(jnp.max(s_band, axis=-1, keepdims=True),
                                jnp.max(s_ctx, axis=-1, keepdims=True)), sink)
    p_band = jnp.exp(s_band - m)
    p_ctx = jnp.exp(s_ctx - m)
    denom = (jnp.sum(p_band, axis=-1, keepdims=True) + jnp.sum(p_ctx, axis=-1, keepdims=True)
             + jnp.exp(sink - m))
    o = (jnp.dot(p_band.astype(bf16), vb, preferred_element_type=f32)
         + jnp.dot(p_ctx.astype(bf16), vc_ref[0, 0], preferred_element_type=f32))
    o_ref[0] = (o / denom).reshape(SWA_GROUP, SWA_BLOCK, SWA_HEAD_DIM)


def swa_attention(q, k, v, kc, vc, sink):
    bsz, _, seq, hd = q.shape
    n_ctx = kc.shape[2]
    bf16 = jnp.bfloat16
    sink_rows = jnp.repeat(sink.astype(jnp.float32), SWA_BLOCK).reshape(SWA_KV_HEADS, SWA_GROUP * SWA_BLOCK, 1)
    kv_spec = pl.BlockSpec((1, 1, seq, hd), lambda b, h, n: (b, h, 0, 0))
    ctx_spec = pl.BlockSpec((1, 1, n_ctx, hd), lambda b, h, n: (b, h, 0, 0))
    return pl.pallas_call(
        _swa_kernel,
        grid=(bsz, SWA_KV_HEADS, seq // SWA_BLOCK),
        in_specs=[pl.BlockSpec((1, SWA_GROUP, SWA_BLOCK, hd), lambda b, h, n: (b, h, n, 0)),
                  kv_spec, kv_spec, ctx_spec, ctx_spec,
                  pl.BlockSpec((1, SWA_GROUP * SWA_BLOCK, 1), lambda b, h, n: (h, 0, 0))],
        out_specs=pl.BlockSpec((1, SWA_GROUP, SWA_BLOCK, hd), lambda b, h, n: (b, h, n, 0)),
        out_shape=jax.ShapeDtypeStruct((bsz, SWA_HEADS, seq, hd), jnp.float32),
        compiler_params=pltpu.CompilerParams(dimension_semantics=("parallel", "parallel", "parallel")),
        name="swa_attention",
    )(q.astype(bf16), k.astype(bf16), v.astype(bf16), kc.astype(bf16), vc.astype(bf16), sink_rows)


SSD_PAIR = 2 * SSD_HEAD_DIM


def _softplus(x):
    return jnp.maximum(x, 0.0) + jnp.log1p(jnp.exp(-jnp.abs(x)))


def _ssd_kernel(u_ref, dt_ref, dtt_ref, bias_r_ref, bias_c_ref, alog_r_ref, alog_c_ref, dx_ref,
                y_ref, state_ref, *, reverse):
    f32 = jnp.float32
    bf16 = jnp.bfloat16
    q = SSD_CHUNK
    hi = lax.Precision.HIGHEST

    @pl.when(pl.program_id(1) == 0)
    def _():
        state_ref[...] = jnp.zeros_like(state_ref)

    row = lax.broadcasted_iota(jnp.int32, (q, q), 0)
    col = lax.broadcasted_iota(jnp.int32, (q, q), 1)
    before = (col >= row) if reverse else (col <= row)
    tri = before.astype(f32)

    dt = _softplus(dt_ref[0] + bias_r_ref[...])
    dtt = _softplus(dtt_ref[0] + bias_c_ref[...])
    a_r = -jnp.exp(alog_r_ref[...])
    a_c = -jnp.exp(alog_c_ref[...])
    cum = jnp.dot(tri, dt * a_r, precision=hi, preferred_element_type=f32)
    cumt = lax.dot_general(dtt * a_c, tri, (((1,), (1,)), ((), ())), precision=hi,
                           preferred_element_type=f32)
    tot = cum[0:1, :] if reverse else cum[q - 1:q, :]
    ecum = jnp.exp(cum)
    wgt = jnp.exp(tot - cum) * dt
    etot = jnp.exp(tot)

    lane_lo = lax.broadcasted_iota(jnp.int32, (q, SSD_PAIR), 1) < SSD_HEAD_DIM
    lane_lo_row = lane_lo[0:1, :]
    heads_per_group = SSD_HEADS_PER_GROUP
    for g in range(SSD_GROUPS):
        b_off = SSD_D_INNER + g * SSD_STATE
        c_off = SSD_D_INNER + SSD_GROUPS * SSD_STATE + g * SSD_STATE
        bg = u_ref[0, :, b_off:b_off + SSD_STATE].astype(bf16)
        cg = u_ref[0, :, c_off:c_off + SSD_STATE].astype(bf16)
        cb = lax.dot_general(cg, bg, (((1,), (1,)), ((), ())), preferred_element_type=f32)
        for kk in range(heads_per_group // 2):
            k = g * (heads_per_group // 2) + kk
            lanes = slice(k * SSD_PAIR, (k + 1) * SSD_PAIR)
            xp = u_ref[0, :, lanes]
            y = dx_ref[:, lanes] * xp
            for half in range(2):
                hh = 2 * k + half
                seg = cum[:, hh:hh + 1] - cumt[hh:hh + 1, :]
                decay = jnp.where(before, jnp.exp(seg), 0.0) * dtt[hh:hh + 1, :]
                m = (cb * decay).astype(bf16)
                xm = jnp.where(lane_lo if half == 0 else jnp.logical_not(lane_lo), xp, 0.0).astype(bf16)
                y = y + jnp.dot(m, xm, preferred_element_type=f32)
            e_pair = jnp.where(lane_lo, ecum[:, 2 * k:2 * k + 1], ecum[:, 2 * k + 1:2 * k + 2])
            w_pair = jnp.where(lane_lo, wgt[:, 2 * k:2 * k + 1], wgt[:, 2 * k + 1:2 * k + 2])
            t_pair = jnp.where(lane_lo_row, etot[:, 2 * k:2 * k + 1], etot[:, 2 * k + 1:2 * k + 2])
            prev = state_ref[:, lanes]
            y = y + jnp.dot(cg, prev.astype(bf16), preferred_element_type=f32) * e_pair
            y_ref[0, :, lanes] = y
            xw = (xp * w_pair).astype(bf16)
            s_new = lax.dot_general(bg, xw, (((0,), (0,)), ((), ())), preferred_element_type=f32)
            state_ref[:, lanes] = prev * t_pair + s_new


def ssd_scan_dir(u, dt_raw, bias, a_log, d_skip, reverse, n_ctx):
    bsz, t_all, _ = u.shape
    q = SSD_CHUNK
    nc = t_all // q
    ncc = n_ctx // q
    if reverse:
        def chunk(c):
            return jnp.where(c < ncc, ncc - 1 - c, nc + ncc - 1 - c)
    else:
        def chunk(c):
            return c
    f32 = jnp.float32
    row = lambda v: v.astype(f32).reshape(1, SSD_HEADS)
    colv = lambda v: v.astype(f32).reshape(SSD_HEADS, 1)
    small_r = pl.BlockSpec((1, SSD_HEADS), lambda b, c: (0, 0))
    small_c = pl.BlockSpec((SSD_HEADS, 1), lambda b, c: (0, 0))
    return pl.pallas_call(
        functools.partial(_ssd_kernel, reverse=reverse),
        grid=(bsz, nc),
        in_specs=[pl.BlockSpec((1, q, SSD_CONV_CH), lambda b, c: (b, chunk(c), 0)),
                  pl.BlockSpec((1, q, SSD_HEADS), lambda b, c: (b, chunk(c), 0)),
                  pl.BlockSpec((1, SSD_HEADS, q), lambda b, c: (b, 0, chunk(c))),
                  small_r, small_c, small_r, small_c,
                  pl.BlockSpec((1, SSD_D_INNER), lambda b, c: (0, 0))],
        out_specs=pl.BlockSpec((1, q, SSD_D_INNER), lambda b, c: (b, chunk(c), 0)),
        out_shape=jax.ShapeDtypeStruct((bsz, t_all, SSD_D_INNER), f32),
        scratch_shapes=[pltpu.VMEM((SSD_STATE, SSD_D_INNER), f32)],
        compiler_params=pltpu.CompilerParams(dimension_semantics=("parallel", "arbitrary")),
        name="ssd_scan_bwd" if reverse else "ssd_scan_fwd",
    )(u, dt_raw, jnp.swapaxes(dt_raw, 1, 2), row(bias), colv(bias), row(a_log), colv(a_log),
      jnp.repeat(d_skip.astype(f32), SSD_HEAD_DIM).reshape(1, SSD_D_INNER))


PROJ_TM = 256
SWA_SLOTS = SWA_HEADS + 2 * SWA_KV_HEADS
QKV_W = SWA_SLOTS * LANES
CQ_W = MLA_Q_RANK + MLA_KV_RANK
MISC_W = LANES
MISC_ROPE_AT = MLA_NOPE


def _rms(x):
    return x * lax.rsqrt(jnp.mean(x * x, axis=-1, keepdims=True) + EPS)


def _slot_norm_rope(x, width, gain, rope, half):
    y = x * lax.rsqrt(jnp.sum(x * x, axis=-1, keepdims=True) * (1.0 / width) + EPS) * gain
    if rope is None:
        return y
    c, s_lo, s_hi = rope
    return y * c + pltpu.roll(y, LANES - half, 1) * s_lo + pltpu.roll(y, half, 1) * s_hi


def _premix_kernel(*refs, with_rope):
    f32 = jnp.float32
    bf16 = jnp.bfloat16
    n_rope = 6 if with_rope else 0
    (x_ref, g_ref, sc_ref, sh_ref, bg_ref, wz_ref, wx_ref, wq_ref, wc_ref, wg_ref, wm_ref,
     wuq_ref, wuk_ref, wuv_ref, gsq_ref, gsk_ref, gqa_ref, gkva_ref, gmq_ref, gmk_ref) = refs[:20]
    rope_refs = refs[20:20 + n_rope]
    (z_ref, xbc_ref, gate_ref, misc_ref, sq_ref, sk_ref, sv_ref, mq_ref, mk_ref, mv_ref) = refs[20 + n_rope:]
    swa_rope = tuple(r[...] for r in rope_refs[0:3]) if with_rope else None
    mla_rope = tuple(r[...] for r in rope_refs[3:6]) if with_rope else None

    h = (_rms(x_ref[...]) * g_ref[...] * (1.0 + sc_ref[0]) + sh_ref[0]).astype(bf16)
    z_ref[...] = jnp.dot(h, wz_ref[...], preferred_element_type=f32)
    xbc_ref[...] = jnp.dot(h, wx_ref[...], preferred_element_type=f32)
    gate_ref[...] = jax.nn.sigmoid(jnp.dot(h, wg_ref[...], preferred_element_type=f32) + bg_ref[...])
    misc = jnp.dot(h, wm_ref[...], preferred_element_type=f32)
    misc_ref[...] = misc

    qkv = jnp.dot(h, wq_ref[...], preferred_element_type=f32)
    slot = lambda a, i: a[:, i * LANES:(i + 1) * LANES]
    for hd in range(SWA_HEADS):
        y = _slot_norm_rope(slot(qkv, hd), SWA_HEAD_DIM, gsq_ref[...], swa_rope, SWA_HEAD_DIM // 2)
        sq_ref[0, hd] = y[:, :SWA_HEAD_DIM].astype(bf16)
    for hd in range(SWA_KV_HEADS):
        y = _slot_norm_rope(slot(qkv, SWA_HEADS + hd), SWA_HEAD_DIM, gsk_ref[...], swa_rope, SWA_HEAD_DIM // 2)
        sk_ref[0, hd] = y[:, :SWA_HEAD_DIM].astype(bf16)
        sv_ref[0, hd] = slot(qkv, SWA_HEADS + SWA_KV_HEADS + hd)[:, :SWA_HEAD_DIM].astype(bf16)

    cq = jnp.dot(h, wc_ref[...], preferred_element_type=f32)
    qa = (_rms(cq[:, :MLA_Q_RANK]) * gqa_ref[...]).astype(bf16)
    kva = (_rms(cq[:, MLA_Q_RANK:]) * gkva_ref[...]).astype(bf16)
    q_up = jnp.dot(qa, wuq_ref[...], preferred_element_type=f32)
    k_up = jnp.dot(kva, wuk_ref[...], preferred_element_type=f32)
    v_up = jnp.dot(kva, wuv_ref[...], preferred_element_type=f32)
    lane = lax.broadcasted_iota(jnp.int32, misc.shape, 1)
    k_rope = jnp.where((lane >= MISC_ROPE_AT) & (lane < MISC_ROPE_AT + MLA_ROPE), misc, 0.0)
    for hd in range(MLA_HEADS):
        mq_ref[0, hd] = _slot_norm_rope(slot(q_up, hd), MLA_QK, gmq_ref[...], mla_rope,
                                        MLA_ROPE // 2).astype(bf16)
        mk_ref[0, hd] = _slot_norm_rope(slot(k_up, hd) + k_rope, MLA_QK, gmk_ref[...], mla_rope,
                                        MLA_ROPE // 2).astype(bf16)
        mv_ref[0, hd] = slot(v_up, hd)[:, :MLA_V].astype(bf16)


def _slot_pack(w, n_heads, width):
    k = w.shape[0]
    w3 = w.reshape(k, n_heads, width)
    return jnp.pad(w3, ((0, 0), (0, 0), (0, LANES - width))).reshape(k, n_heads * LANES)


def _slot_row(v, width_to=LANES):
    v = v.astype(jnp.float32)
    return jnp.pad(v, (0, width_to - v.shape[0])).reshape(1, width_to)


def _pack_weights(p):
    w_in = p['w_in']
    pts = [0] + split_points() + [IN_WIDTH]
    col = lambda i: w_in[:, pts[i]:pts[i + 1]]
    bf16 = jnp.bfloat16
    zeros = jnp.zeros((w_in.shape[0], MISC_ROPE_AT), w_in.dtype)
    qkv = jnp.concatenate([_slot_pack(col(4), SWA_HEADS, SWA_HEAD_DIM),
                           _slot_pack(col(5), SWA_KV_HEADS, SWA_HEAD_DIM),
                           _slot_pack(col(6), SWA_KV_HEADS, SWA_HEAD_DIM)], axis=1)
    ukv = p['mla_w_ukv'].reshape(MLA_KV_RANK, MLA_HEADS, MLA_NOPE + MLA_V)
    weights = (col(0), col(1), qkv, jnp.concatenate([col(7), col(8)], axis=1),
               jnp.concatenate([col(10), col(11), col(12)], axis=1),
               jnp.concatenate([zeros, col(9), col(2), col(3)], axis=1),
               _slot_pack(p['mla_w_uq'], MLA_HEADS, MLA_QK),
               _slot_pack(ukv[:, :, :MLA_NOPE].reshape(MLA_KV_RANK, -1), MLA_HEADS, MLA_NOPE),
               _slot_pack(ukv[:, :, MLA_NOPE:].reshape(MLA_KV_RANK, -1), MLA_HEADS, MLA_V))
    gains = (_slot_row(p['swa_q_norm'] * SWA_SCALE), _slot_row(p['swa_k_norm']),
             _slot_row(p['mla_q_a_norm'], MLA_Q_RANK), _slot_row(p['mla_kv_a_norm'], MLA_KV_RANK),
             _slot_row(p['mla_q_norm'] * MLA_SCALE), _slot_row(p['mla_k_norm']))
    return tuple(w.astype(bf16) for w in weights), gains


def _rope_slot_tables(cos, sin, start):
    n, half = cos.shape
    ones = lambda w: jnp.ones((n, w), jnp.float32)
    zeros = lambda w: jnp.zeros((n, w), jnp.float32)
    tail = LANES - start - 2 * half
    c = jnp.concatenate([ones(start), cos, cos, ones(tail)], axis=1)
    s_lo = jnp.concatenate([zeros(start), -sin, zeros(half), zeros(tail)], axis=1)
    s_hi = jnp.concatenate([zeros(start), zeros(half), sin, zeros(tail)], axis=1)
    return c, s_lo, s_hi


def _batch_row_spec(width, tiles_per_batch):
    return pl.BlockSpec((1, 1, width), lambda t: (t // tiles_per_batch, 0, 0))


def premix(x, norm_g, scale, shift, b_gate, packed, rope_tables):
    bsz, t_len, d = x.shape
    n = bsz * t_len
    tm = min(PROJ_TM, t_len)
    tpb = t_len // tm
    f32 = jnp.float32
    bf16 = jnp.bfloat16
    weights, gains = packed
    const = lambda a: pl.BlockSpec(a.shape, lambda t: (0,) * a.ndim)
    tok = lambda w: pl.BlockSpec((tm, w), lambda t: (t, 0))
    heads = lambda nh, w: pl.BlockSpec((1, nh, tm, w), lambda t: (t // tpb, 0, t % tpb, 0))
    with_rope = rope_tables is not None
    rope_in = list(rope_tables) if with_rope else []
    rope_specs = [pl.BlockSpec((tm, LANES), lambda t: (t % tpb, 0)) for _ in rope_in]
    flat_w = (SSD_D_INNER, SSD_CONV_CH, 3 * D_MODEL, MISC_W)
    head_shapes = ((SWA_HEADS, SWA_HEAD_DIM), (SWA_KV_HEADS, SWA_HEAD_DIM), (SWA_KV_HEADS, SWA_HEAD_DIM),
                   (MLA_HEADS, LANES), (MLA_HEADS, LANES), (MLA_HEADS, MLA_V))
    small = [norm_g.reshape(1, d), scale, shift, b_gate.reshape(1, 3 * D_MODEL)]
    outs = pl.pallas_call(
        functools.partial(_premix_kernel, with_rope=with_rope),
        grid=(n // tm,),
        in_specs=([tok(d), const(small[0]), _batch_row_spec(d, tpb), _batch_row_spec(d, tpb), const(small[3])]
                  + [const(w) for w in weights] + [const(g) for g in gains] + rope_specs),
        out_specs=[tok(w) for w in flat_w] + [heads(nh, w) for nh, w in head_shapes],
        out_shape=([jax.ShapeDtypeStruct((n, w), f32) for w in flat_w]
                   + [jax.ShapeDtypeStruct((bsz, nh, t_len, w), bf16) for nh, w in head_shapes]),
        compiler_params=pltpu.CompilerParams(dimension_semantics=("parallel",),
                                             vmem_limit_bytes=PEER_VMEM_LIMIT),
        name="premix",
    )(x.reshape(n, d), *small, *weights, *gains, *rope_in)
    return [o.reshape(bsz, t_len, -1) for o in outs[:4]] + list(outs[4:])


def _heads_to_lanes(y_ref):
    return jnp.concatenate([y_ref[0, h].astype(jnp.bfloat16) for h in range(y_ref.shape[1])], axis=-1)


def _merge_kernel(x_ref, yf_ref, yb_ref, z_ref, yw_ref, ym_ref, gate_ref, gs_ref, g1_ref, n2_ref, sc_ref,
                  sh_ref, w1_ref, w2_ref, w3_ref, wo_ref, xo_ref, h2_ref):
    f32 = jnp.float32
    bf16 = jnp.bfloat16
    d = D_MODEL
    z = z_ref[...]
    ys = _rms((yf_ref[...] + yb_ref[...]) * (z * jax.nn.sigmoid(z))) * gs_ref[...]
    m = (gate_ref[:, 0:d] * jnp.dot(ys.astype(bf16), w1_ref[...], preferred_element_type=f32)
         + gate_ref[:, d:2 * d] * jnp.dot(_heads_to_lanes(yw_ref), w2_ref[...], preferred_element_type=f32)
         + gate_ref[:, 2 * d:3 * d] * jnp.dot(_heads_to_lanes(ym_ref), w3_ref[...], preferred_element_type=f32))
    mix = jnp.dot(m.astype(bf16), wo_ref[...], preferred_element_type=f32)
    xn = x_ref[...] + g1_ref[0] * mix
    xo_ref[...] = xn
    h2_ref[...] = (_rms(xn) * n2_ref[...] * (1.0 + sc_ref[0]) + sh_ref[0]).astype(bf16)


def merge_norm(x, yf, yb, y_block, z, yw, ym, gates, p, g1, sc2, sh2):
    bsz, t_len, d = x.shape
    n = bsz * t_len
    tm = min(PROJ_TM, t_len)
    bf16 = jnp.bfloat16
    const = lambda shape: pl.BlockSpec(shape, lambda t: (0,) * len(shape))
    tok = lambda w: pl.BlockSpec((tm, w), lambda t: (t, 0))
    brow = _batch_row_spec(d, t_len // tm)
    ytok = pl.BlockSpec((tm, d), lambda t: (y_block(t, tm), 0))
    tpb = t_len // tm
    heads = lambda a: pl.BlockSpec((1, a.shape[1], tm, a.shape[3]), lambda t: (t // tpb, 0, t % tpb, 0))
    flat = lambda a: a.reshape(-1, a.shape[-1])
    swa_w = SWA_HEADS * SWA_HEAD_DIM
    mla_w = MLA_HEADS * MLA_V
    xo, h2 = pl.pallas_call(
        _merge_kernel,
        grid=(n // tm,),
        in_specs=[tok(d), ytok, ytok, tok(d), heads(yw), heads(ym), tok(3 * d),
                  const((1, d)), brow, const((1, d)), brow, brow,
                  const((d, d)), const((swa_w, d)), const((mla_w, d)), const((d, d))],
        out_specs=[tok(d), tok(d)],
        out_shape=[jax.ShapeDtypeStruct((n, d), jnp.float32), jax.ShapeDtypeStruct((n, d), bf16)],
        compiler_params=pltpu.CompilerParams(dimension_semantics=("parallel",),
                                             vmem_limit_bytes=PEER_VMEM_LIMIT),
        name="merge_norm",
    )(flat(x), flat(yf), flat(yb), flat(z), yw, ym, flat(gates),
      p['ssd_norm_g'].reshape(1, d), g1, p['norm2_g'].reshape(1, d), sc2, sh2,
      p['w_br_ssd'].astype(bf16), p['w_br_swa'].astype(bf16), p['w_br_mla'].astype(bf16),
      p['w_out'].astype(bf16))
    return xo.reshape(bsz, t_len, d), h2.reshape(bsz, t_len, d)


def split_points():
    pts, acc = [], 0
    for s in IN_SIZES[:-1]:
        acc += s
        pts.append(acc)
    return pts


def rms_norm(x, g):
    xf = x.astype(jnp.float32)
    y = xf * lax.rsqrt(jnp.mean(xf * xf, axis=-1, keepdims=True) + EPS)
    return (y * g.astype(jnp.float32)).astype(x.dtype)


def axial_rope(rows, dim):
    pairs = dim // 4
    freqs = ROPE_BASE ** (-jnp.arange(pairs, dtype=jnp.float32) / pairs)
    pos_r = jnp.repeat(jnp.arange(rows, dtype=jnp.float32), GRID_W)
    pos_c = jnp.tile(jnp.arange(GRID_W, dtype=jnp.float32), rows)
    ang = jnp.concatenate([pos_r[:, None] * freqs, pos_c[:, None] * freqs], axis=-1)
    return jnp.cos(ang), jnp.sin(ang)


def apply_rope(x, cos, sin):
    d2 = x.shape[-1] // 2
    x1, x2 = x[..., :d2], x[..., d2:]
    cs, sn = cos[:, None, :], sin[:, None, :]
    return jnp.concatenate([x1 * cs - x2 * sn, x2 * cs + x1 * sn], axis=-1).astype(x.dtype)


def dwconv_centred(u, w, b):
    pad = (SSD_CONV - 1) // 2
    y = lax.conv_general_dilated(u, w[:, None, :].astype(u.dtype), window_strides=(1,),
                                 padding=[(pad, pad)], dimension_numbers=('NWC', 'WIO', 'NWC'),
                                 feature_group_count=u.shape[-1])
    return y + b


def ssd_mixer(lat, ctx, p, with_ctx_out):
    f32 = jnp.float32

    def conv_act(xbc):
        return jax.nn.silu(dwconv_centred(xbc, p['ssd_conv_w'], p['ssd_conv_b']))

    n_ctx = ctx[1].shape[1]
    u_all = jnp.concatenate([conv_act(ctx[1]), conv_act(lat[1])], axis=1)
    ys = []
    for direction in range(2):
        dt_raw = jnp.concatenate([ctx[2 + direction], lat[2 + direction]], axis=1)
        ys.append(ssd_scan_dir(u_all, dt_raw, p['ssd_dt_bias'][direction], p['ssd_a_log'][direction],
                               p['ssd_d'][direction], direction == 1, n_ctx))
    return ys


def swa_mixer(lat, ctx, p, with_ctx_out):
    ql, kl, vl = lat
    qc, kc, vc = ctx
    y_l = swa_attention(ql, kl, vl, kc, vc, p['swa_sink'])
    y_c = None
    if with_ctx_out:
        n_ctx = qc.shape[2]
        y_c = flash_attention(qc, kc, vc, n_ctx, n_ctx, sink=p['swa_sink'])
    return y_l, y_c


def mla_mixer(lat, ctx, with_ctx_out):
    ql, kl, vl = lat
    qc, kc, vc = ctx
    k_all = jnp.concatenate([kc, kl], axis=2)
    v_all = jnp.concatenate([vc, vl], axis=2)
    y_l = flash_attention(ql, k_all, v_all, MLA_TQ, MLA_TK)
    y_c = None
    if with_ctx_out:
        n_ctx = qc.shape[2]
        y_c = flash_attention(qc, kc, vc, n_ctx, n_ctx)
    return y_l, y_c


PEER_ROUTE_TM = 256
PEER_TM = 512
PEER_EC = 512
PEER_VMEM_LIMIT = 56 * 1024 * 1024
_SQRT_HALF = 0.7071067811865476


def _top16_rows(s):
    rows = []
    for _ in range(PEER_TOPK):
        m = jnp.max(s, axis=0, keepdims=True)
        rows.append(m)
        s = jnp.where(s == m, -jnp.inf, s)
    return jnp.concatenate(rows, axis=0)


def _peer_route_kernel(h_ref, wq_ref, keys_ref, s0_ref, s1_ref, a_ref, b_ref, tau_ref, q_scr):
    f32 = jnp.float32
    q = jnp.dot(h_ref[...].astype(jnp.bfloat16), wq_ref[...], preferred_element_type=f32)
    q_scr[...] = q.astype(jnp.bfloat16)

    def head_body(hd, carry):
        def scores(p):
            col = pl.multiple_of((hd * 2 + p) * PEER_HALF, PEER_HALF)
            qs = q_scr[:, pl.ds(col, PEER_HALF)]
            return lax.dot_general(keys_ref[p * PEER_HEADS + hd], qs, (((1,), (1,)), ((), ())),
                                   preferred_element_type=f32)

        s0 = scores(0)
        s1 = scores(1)
        a_top = _top16_rows(s0)
        b_top = _top16_rows(s1)
        cand = [a_top[0:1] + b_top]
        cand += [a_top[r:r + 1] + b_top[0:8] for r in range(1, 8)]
        cand.append(a_top[8:16] + b_top[0:1])
        best = _top16_rows(jnp.concatenate(cand, axis=0))
        z = jnp.sum(jnp.exp(best - best[0:1]), axis=0, keepdims=True)
        a = jnp.exp(s0 - a_top[0:1])
        b = jnp.exp(s1 - b_top[0:1]) / z
        tau = best[PEER_TOPK - 1:PEER_TOPK]
        for tl in range(s0.shape[1] // LANES):
            ts = slice(tl * LANES, (tl + 1) * LANES)
            s0_ref[hd, tl] = s0[:, ts]
            s1_ref[hd, tl] = s1[:, ts]
            a_ref[hd, tl] = a[:, ts]
            b_ref[hd, tl] = b[:, ts]
            tau_ref[hd, tl] = tau[:, ts]
        return carry

    lax.fori_loop(0, PEER_HEADS, head_body, 0)


def _gelu_exact(x):
    return 0.5 * x * (1.0 + lax.erf(x * _SQRT_HALF))


def _peer_expert_kernel(h_ref, u_ref, vt_ref, s0_ref, s1_ref, a_ref, b_ref, tau_ref, x_ref, g2_ref,
                        o_ref, acc_ref, act_scr, g_scr):
    f32 = jnp.float32
    j = pl.program_id(1)
    rows_per_step = PEER_EC // PEER_N_KEYS

    @pl.when(j == 0)
    def _():
        acc_ref[...] = jnp.zeros_like(acc_ref)

    act_scr[...] = lax.dot_general(u_ref[...], h_ref[...], (((1,), (1,)), ((), ())),
                                   preferred_element_type=f32)
    n_tl = PEER_TM // LANES

    def tile_body(t, carry):
        ii = t // n_tl
        tl = t % n_tl
        i = j * rows_per_step + ii
        w = jnp.zeros((PEER_N_KEYS, LANES), f32)
        for hd in range(PEER_HEADS):
            s0_row = s0_ref[hd, tl, pl.ds(i, 1), :]
            a_row = a_ref[hd, tl, pl.ds(i, 1), :]
            keep = (s1_ref[hd, tl] + s0_row) >= tau_ref[hd, tl]
            w = w + jnp.where(keep, b_ref[hd, tl], 0.0) * a_row
        es = pl.ds(pl.multiple_of(ii * PEER_N_KEYS, PEER_N_KEYS), PEER_N_KEYS)
        ts = pl.ds(pl.multiple_of(tl * LANES, LANES), LANES)
        g_scr[es, ts] = (w * _gelu_exact(act_scr[es, ts])).astype(jnp.bfloat16)
        return carry

    lax.fori_loop(0, rows_per_step * n_tl, tile_body, 0)
    acc_ref[...] += jnp.dot(vt_ref[...], g_scr[...], preferred_element_type=f32)

    @pl.when(j == pl.num_programs(1) - 1)
    def _():
        o_ref[...] = x_ref[...] + g2_ref[0] * acc_ref[...].T


def peer_ffn(h, x, g2_tiles, w_q, sub_keys, u_tab, v_tab):
    bsz, T, D = h.shape
    n_tok = bsz * T
    n_exp = u_tab.shape[0]
    hf = h.reshape(n_tok, D)
    f32 = jnp.float32
    bf16 = jnp.bfloat16
    keys = sub_keys.reshape(2 * PEER_HEADS, PEER_N_KEYS, PEER_HALF).astype(bf16)
    tm = PEER_ROUTE_TM
    n_tiles = n_tok // LANES
    big = jax.ShapeDtypeStruct((PEER_HEADS, n_tiles, PEER_N_KEYS, LANES), f32)
    small = jax.ShapeDtypeStruct((PEER_HEADS, n_tiles, 1, LANES), f32)
    big_spec = pl.BlockSpec((PEER_HEADS, tm // LANES, PEER_N_KEYS, LANES), lambda t: (0, t, 0, 0))
    small_spec = pl.BlockSpec((PEER_HEADS, tm // LANES, 1, LANES), lambda t: (0, t, 0, 0))
    s0, s1, a, b, tau = pl.pallas_call(
        _peer_route_kernel,
        grid=(n_tok // tm,),
        in_specs=[pl.BlockSpec((tm, D), lambda t: (t, 0)),
                  pl.BlockSpec((D, PEER_HEADS * PEER_D_KEY), lambda t: (0, 0)),
                  pl.BlockSpec((2 * PEER_HEADS, PEER_N_KEYS, PEER_HALF), lambda t: (0, 0, 0))],
        out_specs=[big_spec, big_spec, big_spec, big_spec, small_spec],
        out_shape=[big, big, big, big, small],
        scratch_shapes=[pltpu.VMEM((tm, PEER_HEADS * PEER_D_KEY), bf16)],
        compiler_params=pltpu.CompilerParams(dimension_semantics=("parallel",),
                                             vmem_limit_bytes=PEER_VMEM_LIMIT),
        name="peer_route",
    )(hf, w_q.astype(bf16), keys)

    tm = PEER_TM
    ec = PEER_EC
    big_spec = pl.BlockSpec((PEER_HEADS, tm // LANES, PEER_N_KEYS, LANES), lambda t, j: (0, t, 0, 0))
    small_spec = pl.BlockSpec((PEER_HEADS, tm // LANES, 1, LANES), lambda t, j: (0, t, 0, 0))
    out = pl.pallas_call(
        _peer_expert_kernel,
        grid=(n_tok // tm, n_exp // ec),
        in_specs=[pl.BlockSpec((tm, D), lambda t, j: (t, 0)),
                  pl.BlockSpec((ec, D), lambda t, j: (j, 0)),
                  pl.BlockSpec((D, ec), lambda t, j: (0, j)),
                  big_spec, big_spec, big_spec, big_spec, small_spec,
                  pl.BlockSpec((tm, D), lambda t, j: (t, 0)),
                  pl.BlockSpec((1, 1, D), lambda t, j: (t, 0, 0))],
        out_specs=pl.BlockSpec((tm, D), lambda t, j: (t, 0)),
        out_shape=jax.ShapeDtypeStruct((n_tok, D), f32),
        scratch_shapes=[pltpu.VMEM((D, tm), f32), pltpu.VMEM((ec, tm), f32), pltpu.VMEM((ec, tm), bf16)],
        compiler_params=pltpu.CompilerParams(dimension_semantics=("parallel", "arbitrary"),
                                             vmem_limit_bytes=PEER_VMEM_LIMIT),
        name="peer_expert",
    )(hf, u_tab.astype(bf16), v_tab.T.astype(bf16), s0, s1, a, b, tau, x.reshape(n_tok, D), g2_tiles)
    return out.reshape(bsz, T, D)


def _split_cols(t, sizes):
    out, acc = [], 0
    for s in sizes:
        out.append(t[..., acc:acc + s])
        acc += s
    return out


def layer(xl, xc, c, c_ctx, p, rope_tables, with_ctx_out):
    bsz, seq, d = xl.shape
    n_ctx = xc.shape[1]
    mod_l = (jax.nn.silu(c) @ p['w_mod'] + p['b_mod'])[:, None, :]
    mod_c = jnp.broadcast_to((jax.nn.silu(c_ctx) @ p['w_mod'] + p['b_mod'])[None, None, :], mod_l.shape)
    sh1_l, sc1_l, g1_l, sh2_l, sc2_l, g2_l = jnp.split(mod_l, 6, axis=-1)
    sh1_c, sc1_c, g1_c, sh2_c, sc2_c, g2_c = jnp.split(mod_c, 6, axis=-1)

    packed = _pack_weights(p)
    dt_at = MISC_ROPE_AT + MLA_ROPE

    def project(x, sc, sh, rope):
        z, xbc, gates, misc, sq, sk, sv, mq, mk, mv = premix(x, p['norm1_g'], sc, sh, p['b_gate'], packed, rope)
        dtf = misc[..., dt_at:dt_at + SSD_HEADS]
        dtb = misc[..., dt_at + SSD_HEADS:dt_at + 2 * SSD_HEADS]
        return dict(z=z, gates=gates, ssd=(None, xbc, dtf, dtb), swa=(sq, sk, sv), mla=(mq, mk, mv))

    pl_ = project(xl, sc1_l, sh1_l, rope_tables)
    pc = project(xc, sc1_c, sh1_c, None)
    yf, yb = ssd_mixer(pl_['ssd'], pc['ssd'], p, with_ctx_out)
    y_swa_l, y_swa_c = swa_mixer(pl_['swa'], pc['swa'], p, with_ctx_out)
    y_mla_l, y_mla_c = mla_mixer(pl_['mla'], pc['mla'], with_ctx_out)

    def lat_block(t, tm):
        return (t // (seq // tm)) * ((n_ctx + seq) // tm) + n_ctx // tm + t % (seq // tm)

    def ctx_block(t, tm):
        return (t // (n_ctx // tm)) * ((n_ctx + seq) // tm) + t % (n_ctx // tm)

    peer = (p['peer_w_q'], p['peer_keys'], p['peer_u'], p['peer_v'])
    xl, h2_l = merge_norm(xl, yf, yb, lat_block, pl_['z'], y_swa_l, y_mla_l, pl_['gates'], p,
                          g1_l, sc2_l, sh2_l)
    if with_ctx_out:
        xc, h2_c = merge_norm(xc, yf, yb, ctx_block, pc['z'], y_swa_c, y_mla_c, pc['gates'], p,
                              g1_c, sc2_c, sh2_c)
        g2_c_tiles = jnp.broadcast_to(g2_c[:1], (bsz * n_ctx // PEER_TM, 1, d))
        xc = peer_ffn(h2_c, xc, g2_c_tiles, *peer)
    xl = peer_ffn(h2_l, xl, jnp.repeat(g2_l, seq // PEER_TM, axis=0), *peer)
    return xl, xc


def kernel(x, c, ctx, c_ctx, w_mod, b_mod, norm1_g, norm2_g, w_in, ssd_conv_w, ssd_conv_b, ssd_dt_bias,
           ssd_a_log, ssd_d, ssd_norm_g, swa_q_norm, swa_k_norm, swa_sink, mla_q_a_norm, mla_kv_a_norm,
           mla_w_uq, mla_w_ukv, mla_q_norm, mla_k_norm, b_gate, w_br_ssd, w_br_swa, w_br_mla, w_out,
           peer_w_q, peer_keys, peer_u, peer_v):
    L = x.shape[1]
    rows = L // GRID_W
    rope_tables = (_rope_slot_tables(*axial_rope(rows, SWA_HEAD_DIM), 0)
                   + _rope_slot_tables(*axial_rope(rows, MLA_ROPE), MLA_NOPE))
    params = dict(w_mod=w_mod, b_mod=b_mod, norm1_g=norm1_g, norm2_g=norm2_g, w_in=w_in,
                  ssd_conv_w=ssd_conv_w, ssd_conv_b=ssd_conv_b, ssd_dt_bias=ssd_dt_bias,
                  ssd_a_log=ssd_a_log, ssd_d=ssd_d, ssd_norm_g=ssd_norm_g, swa_q_norm=swa_q_norm,
                  swa_k_norm=swa_k_norm, swa_sink=swa_sink, mla_q_a_norm=mla_q_a_norm,
                  mla_kv_a_norm=mla_kv_a_norm, mla_w_uq=mla_w_uq, mla_w_ukv=mla_w_ukv,
                  mla_q_norm=mla_q_norm, mla_k_norm=mla_k_norm, b_gate=b_gate, w_br_ssd=w_br_ssd,
                  w_br_swa=w_br_swa, w_br_mla=w_br_mla, w_out=w_out, peer_w_q=peer_w_q,
                  peer_keys=peer_keys, peer_u=peer_u, peer_v=peer_v)
    xl, xc = x, ctx
    for i in range(DEPTH):
        p = {k: v[i] for k, v in params.items()}
        xl, xc = layer(xl, xc, c, c_ctx, p, rope_tables, i < DEPTH - 1)
    return xl
```

```python
import functools

import jax
import jax.numpy as jnp
from jax import lax
from jax.experimental import pallas as pl
from jax.experimental.pallas import tpu as pltpu

D_MODEL = 1024
DEPTH = 2
GRID_W = 64
EPS = 1e-6
NEG_INF = -1e30
ROPE_BASE = 10000.0

SSD_D_INNER = 1024
SSD_HEAD_DIM = 64
SSD_HEADS = 16
SSD_GROUPS = 2
SSD_HEADS_PER_GROUP = 8
SSD_STATE = 64
SSD_CONV = 5
SSD_CHUNK = 128
SSD_CONV_CH = 1280

SWA_HEADS = 8
SWA_KV_HEADS = 2
SWA_HEAD_DIM = 64
SWA_WINDOW = 128
SWA_BLOCK = 128
SWA_SCALE = SWA_HEAD_DIM ** -0.5

MLA_HEADS = 8
MLA_Q_RANK = 384
MLA_KV_RANK = 256
MLA_NOPE = 64
MLA_ROPE = 32
MLA_V = 64
MLA_QK = MLA_NOPE + MLA_ROPE
MLA_BLOCK = 128
MLA_SCALE = MLA_QK ** -0.5
MLA_TQ = 1024
MLA_TK = 768

PEER_HEADS = 8
PEER_N_KEYS = 128
PEER_D_KEY = 256
PEER_HALF = 128
PEER_TOPK = 16
PEER_BLOCK = 64

IN_SIZES = (SSD_D_INNER, SSD_CONV_CH, SSD_HEADS, SSD_HEADS,
            SWA_HEADS * SWA_HEAD_DIM, SWA_KV_HEADS * SWA_HEAD_DIM, SWA_KV_HEADS * SWA_HEAD_DIM,
            MLA_Q_RANK, MLA_KV_RANK, MLA_ROPE,
            D_MODEL, D_MODEL, D_MODEL)
IN_WIDTH = sum(IN_SIZES)

LANES = 128


def _round_up(n, m):
    return (n + m - 1) // m * m


def _mm_kernel(a_ref, b_ref, o_ref):
    o_ref[...] = jnp.dot(a_ref[...].astype(jnp.bfloat16), b_ref[...],
                         preferred_element_type=jnp.float32)


def _pick_tile(n, cap, unit):
    t = min(n, cap)
    while n % t or t % unit:
        t -= unit
    return t


def pmatmul(a, w):
    m, k = a.shape
    n = w.shape[1]
    n_pad = _round_up(n, LANES)
    wb = w.astype(jnp.bfloat16)
    if n_pad != n:
        wb = jnp.pad(wb, ((0, 0), (0, n_pad - n)))
    tm = _pick_tile(m, 512, 8)
    tn = _pick_tile(n_pad, 1024, LANES)
    out = pl.pallas_call(
        _mm_kernel,
        grid=(m // tm, n_pad // tn),
        in_specs=[pl.BlockSpec((tm, k), lambda i, j: (i, 0)),
                  pl.BlockSpec((k, tn), lambda i, j: (0, j))],
        out_specs=pl.BlockSpec((tm, tn), lambda i, j: (i, j)),
        out_shape=jax.ShapeDtypeStruct((m, n_pad), jnp.float32),
        compiler_params=pltpu.CompilerParams(dimension_semantics=("parallel", "parallel")),
        name="matmul",
    )(a, wb)
    return out[:, :n] if n_pad != n else out


def mm(a, w):
    lead = a.shape[:-1]
    return pmatmul(a.reshape(-1, a.shape[-1]), w).reshape(lead + (w.shape[1],))


def _flash_kernel(*refs, tk, has_sink):
    f32 = jnp.float32
    if has_sink:
        q_ref, k_ref, v_ref, sink_ref, o_ref = refs
    else:
        q_ref, k_ref, v_ref, o_ref = refs
    q = q_ref[0, 0]
    tq = q.shape[0]
    dv = v_ref.shape[-1]
    n_chunks = k_ref.shape[2] // tk

    def body(c, carry):
        m, l, acc = carry
        off = pl.multiple_of(c * tk, tk)
        k = k_ref[0, 0, pl.ds(off, tk), :]
        v = v_ref[0, 0, pl.ds(off, tk), :]
        s = lax.dot_general(q, k, (((1,), (1,)), ((), ())), preferred_element_type=f32)
        m_new = jnp.maximum(m, jnp.max(s, axis=-1, keepdims=True))
        p = jnp.exp(s - m_new)
        alpha = jnp.exp(m - m_new)
        l = alpha * l + jnp.sum(p, axis=-1, keepdims=True)
        acc = alpha * acc + jnp.dot(p.astype(jnp.bfloat16), v, preferred_element_type=f32)
        return m_new, l, acc

    if has_sink:
        init = (jnp.broadcast_to(sink_ref[0], (tq, 1)), jnp.ones((tq, 1), f32), jnp.zeros((tq, dv), f32))
    else:
        init = (jnp.full((tq, 1), -jnp.inf, f32), jnp.zeros((tq, 1), f32), jnp.zeros((tq, dv), f32))
    _, l, acc = lax.fori_loop(0, n_chunks, body, init, unroll=True)
    o_ref[0, 0] = acc / l


def flash_attention(q, k, v, tq, tk, sink=None):
    bsz, nh, lq, dq = q.shape
    nk, dv = v.shape[2], v.shape[3]
    grp = nh // k.shape[1]
    bf16 = jnp.bfloat16
    in_specs = [pl.BlockSpec((1, 1, tq, dq), lambda b, h, i: (b, h, i, 0)),
                pl.BlockSpec((1, 1, nk, dq), lambda b, h, i: (b, h // grp, 0, 0)),
                pl.BlockSpec((1, 1, nk, dv), lambda b, h, i: (b, h // grp, 0, 0))]
    args = [q.astype(bf16), k.astype(bf16), v.astype(bf16)]
    if sink is not None:
        in_specs.append(pl.BlockSpec((1, 1, 1), lambda b, h, i: (h, 0, 0)))
        args.append(sink.astype(jnp.float32).reshape(nh, 1, 1))
    return pl.pallas_call(
        functools.partial(_flash_kernel, tk=tk, has_sink=sink is not None),
        grid=(bsz, nh, lq // tq),
        in_specs=in_specs,
        out_specs=pl.BlockSpec((1, 1, tq, dv), lambda b, h, i: (b, h, i, 0)),
        out_shape=jax.ShapeDtypeStruct((bsz, nh, lq, dv), jnp.float32),
        compiler_params=pltpu.CompilerParams(dimension_semantics=("parallel", "parallel", "parallel"),
                                             vmem_limit_bytes=PEER_VMEM_LIMIT),
        name="flash_attention",
    )(*args)


SWA_GROUP = SWA_HEADS // SWA_KV_HEADS
SWA_BAND = 3 * SWA_BLOCK


def _swa_kernel(q_ref, k_ref, v_ref, kc_ref, vc_ref, sink_ref, o_ref):
    f32 = jnp.float32
    bf16 = jnp.bfloat16
    n = pl.program_id(2)
    seq = k_ref.shape[2]
    rows = SWA_GROUP * SWA_BLOCK
    q = q_ref[0].reshape(rows, SWA_HEAD_DIM)
    start = pl.multiple_of(jnp.clip((n - 1) * SWA_BLOCK, 0, seq - SWA_BAND), SWA_BLOCK)
    kb = k_ref[0, 0, pl.ds(start, SWA_BAND), :]
    vb = v_ref[0, 0, pl.ds(start, SWA_BAND), :]
    nt = (((1,), (1,)), ((), ()))
    s_band = lax.dot_general(q, kb, nt, preferred_element_type=f32)
    s_ctx = lax.dot_general(q, kc_ref[0, 0], nt, preferred_element_type=f32)
    qpos = n * SWA_BLOCK + lax.broadcasted_iota(jnp.int32, (rows, SWA_BAND), 0) % SWA_BLOCK
    kpos = start + lax.broadcasted_iota(jnp.int32, (rows, SWA_BAND), 1)
    s_band = jnp.where(jnp.abs(kpos - qpos) <= SWA_WINDOW, s_band, NEG_INF)
    sink = sink_ref[0]
    m = jnp.maximum(jnp.maximum(jnp.max(s_band, axis=-1, keepdims=True),
                                jnp.max(s_ctx, axis=-1, keepdims=True)), sink)
    p_band = jnp.exp(s_band - m)
    p_ctx = jnp.exp(s_ctx - m)
    denom = (jnp.sum(p_band, axis=-1, keepdims=True) + jnp.sum(p_ctx, axis=-1, keepdims=True)
             + jnp.exp(sink - m))
    o = (jnp.dot(p_band.astype(bf16), vb, preferred_element_type=f32)
         + jnp.dot(p_ctx.astype(bf16), vc_ref[0, 0], preferred_element_type=f32))
    o_ref[0] = (o / denom).reshape(SWA_GROUP, SWA_BLOCK, SWA_HEAD_DIM)


def swa_attention(q, k, v, kc, vc, sink):
    bsz, _, seq, hd = q.shape
    n_ctx = kc.shape[2]
    bf16 = jnp.bfloat16
    sink_rows = jnp.repeat(sink.astype(jnp.float32), SWA_BLOCK).reshape(SWA_KV_HEADS, SWA_GROUP * SWA_BLOCK, 1)
    kv_spec = pl.BlockSpec((1, 1, seq, hd), lambda b, h, n: (b, h, 0, 0))
    ctx_spec = pl.BlockSpec((1, 1, n_ctx, hd), lambda b, h, n: (b, h, 0, 0))
    return pl.pallas_call(
        _swa_kernel,
        grid=(bsz, SWA_KV_HEADS, seq // SWA_BLOCK),
        in_specs=[pl.BlockSpec((1, SWA_GROUP, SWA_BLOCK, hd), lambda b, h, n: (b, h, n, 0)),
                  kv_spec, kv_spec, ctx_spec, ctx_spec,
                  pl.BlockSpec((1, SWA_GROUP * SWA_BLOCK, 1), lambda b, h, n: (h, 0, 0))],
        out_specs=pl.BlockSpec((1, SWA_GROUP, SWA_BLOCK, hd), lambda b, h, n: (b, h, n, 0)),
        out_shape=jax.ShapeDtypeStruct((bsz, SWA_HEADS, seq, hd), jnp.float32),
        compiler_params=pltpu.CompilerParams(dimension_semantics=("parallel", "parallel", "parallel")),
        name="swa_attention",
    )(q.astype(bf16), k.astype(bf16), v.astype(bf16), kc.astype(bf16), vc.astype(bf16), sink_rows)


SSD_PAIR = 2 * SSD_HEAD_DIM


def _softplus(x):
    return jnp.maximum(x, 0.0) + jnp.log1p(jnp.exp(-jnp.abs(x)))


def _ssd_kernel(u_ref, dt_ref, dtt_ref, bias_r_ref, bias_c_ref, alog_r_ref, alog_c_ref, dx_ref,
                y_ref, state_ref, *, reverse):
    f32 = jnp.float32
    bf16 = jnp.bfloat16
    q = SSD_CHUNK
    hi = lax.Precision.HIGHEST

    @pl.when(pl.program_id(1) == 0)
    def _():
        state_ref[...] = jnp.zeros_like(state_ref)

    row = lax.broadcasted_iota(jnp.int32, (q, q), 0)
    col = lax.broadcasted_iota(jnp.int32, (q, q), 1)
    before = (col >= row) if reverse else (col <= row)
    tri = before.astype(f32)

    dt = _softplus(dt_ref[0] + bias_r_ref[...])
    dtt = _softplus(dtt_ref[0] + bias_c_ref[...])
    a_r = -jnp.exp(alog_r_ref[...])
    a_c = -jnp.exp(alog_c_ref[...])
    cum = jnp.dot(tri, dt * a_r, precision=hi, preferred_element_type=f32)
    cumt = lax.dot_general(dtt * a_c, tri, (((1,), (1,)), ((), ())), precision=hi,
                           preferred_element_type=f32)
    tot = cum[0:1, :] if reverse else cum[q - 1:q, :]
    ecum = jnp.exp(cum)
    wgt = jnp.exp(tot - cum) * dt
    etot = jnp.exp(tot)

    lane_lo = lax.broadcasted_iota(jnp.int32, (q, SSD_PAIR), 1) < SSD_HEAD_DIM
    lane_lo_row = lane_lo[0:1, :]
    heads_per_group = SSD_HEADS_PER_GROUP
    for g in range(SSD_GROUPS):
        b_off = SSD_D_INNER + g * SSD_STATE
        c_off = SSD_D_INNER + SSD_GROUPS * SSD_STATE + g * SSD_STATE
        bg = u_ref[0, :, b_off:b_off + SSD_STATE].astype(bf16)
        cg = u_ref[0, :, c_off:c_off + SSD_STATE].astype(bf16)
        cb = lax.dot_general(cg, bg, (((1,), (1,)), ((), ())), preferred_element_type=f32)
        for kk in range(heads_per_group // 2):
            k = g * (heads_per_group // 2) + kk
            lanes = slice(k * SSD_PAIR, (k + 1) * SSD_PAIR)
            xp = u_ref[0, :, lanes]
            y = dx_ref[:, lanes] * xp
            for half in range(2):
                hh = 2 * k + half
                seg = cum[:, hh:hh + 1] - cumt[hh:hh + 1, :]
                decay = jnp.where(before, jnp.exp(seg), 0.0) * dtt[hh:hh + 1, :]
                m = (cb * decay).astype(bf16)
                xm = jnp.where(lane_lo if half == 0 else jnp.logical_not(lane_lo), xp, 0.0).astype(bf16)
                y = y + jnp.dot(m, xm, preferred_element_type=f32)
            e_pair = jnp.where(lane_lo, ecum[:, 2 * k:2 * k + 1], ecum[:, 2 * k + 1:2 * k + 2])
            w_pair = jnp.where(lane_lo, wgt[:, 2 * k:2 * k + 1], wgt[:, 2 * k + 1:2 * k + 2])
            t_pair = jnp.where(lane_lo_row, etot[:, 2 * k:2 * k + 1], etot[:, 2 * k + 1:2 * k + 2])
            prev = state_ref[:, lanes]
            y = y + jnp.dot(cg, prev.astype(bf16), preferred_element_type=f32) * e_pair
            y_ref[0, :, lanes] = y
            xw = (xp * w_pair).astype(bf16)
            s_new = lax.dot_general(bg, xw, (((0,), (0,)), ((), ())), preferred_element_type=f32)
            state_ref[:, lanes] = prev * t_pair + s_new


def ssd_scan_dir(u, dt_raw, bias, a_log, d_skip, reverse, n_ctx):
    bsz, t_all, _ = u.shape
    q = SSD_CHUNK
    nc = t_all // q
    ncc = n_ctx // q
    if reverse:
        def chunk(c):
            return jnp.where(c < ncc, ncc - 1 - c, nc + ncc - 1 - c)
    else:
        def chunk(c):
            return c
    f32 = jnp.float32
    row = lambda v: v.astype(f32).reshape(1, SSD_HEADS)
    colv = lambda v: v.astype(f32).reshape(SSD_HEADS, 1)
    small_r = pl.BlockSpec((1, SSD_HEADS), lambda b, c: (0, 0))
    small_c = pl.BlockSpec((SSD_HEADS, 1), lambda b, c: (0, 0))
    return pl.pallas_call(
        functools.partial(_ssd_kernel, reverse=reverse),
        grid=(bsz, nc),
        in_specs=[pl.BlockSpec((1, q, SSD_CONV_CH), lambda b, c: (b, chunk(c), 0)),
                  pl.BlockSpec((1, q, SSD_HEADS), lambda b, c: (b, chunk(c), 0)),
                  pl.BlockSpec((1, SSD_HEADS, q), lambda b, c: (b, 0, chunk(c))),
                  small_r, small_c, small_r, small_c,
                  pl.BlockSpec((1, SSD_D_INNER), lambda b, c: (0, 0))],
        out_specs=pl.BlockSpec((1, q, SSD_D_INNER), lambda b, c: (b, chunk(c), 0)),
        out_shape=jax.ShapeDtypeStruct((bsz, t_all, SSD_D_INNER), f32),
        scratch_shapes=[pltpu.VMEM((SSD_STATE, SSD_D_INNER), f32)],
        compiler_params=pltpu.CompilerParams(dimension_semantics=("parallel", "arbitrary")),
        name="ssd_scan_bwd" if reverse else "ssd_scan_fwd",
    )(u, dt_raw, jnp.swapaxes(dt_raw, 1, 2), row(bias), colv(bias), row(a_log), colv(a_log),
      jnp.repeat(d_skip.astype(f32), SSD_HEAD_DIM).reshape(1, SSD_D_INNER))


PROJ_TM = 256
SWA_SLOTS = SWA_HEADS + 2 * SWA_KV_HEADS
QKV_W = SWA_SLOTS * LANES
CQ_W = MLA_Q_RANK + MLA_KV_RANK
MISC_W = LANES
MISC_ROPE_AT = MLA_NOPE


def _rms(x):
    return x * lax.rsqrt(jnp.mean(x * x, axis=-1, keepdims=True) + EPS)


def _slot_norm_rope(x, width, gain, rope, half):
    y = x * lax.rsqrt(jnp.sum(x * x, axis=-1, keepdims=True) * (1.0 / width) + EPS) * gain
    if rope is None:
        return y
    c, s_lo, s_hi = rope
    return y * c + pltpu.roll(y, LANES - half, 1) * s_lo + pltpu.roll(y, half, 1) * s_hi


def _premix_kernel(*refs, with_rope):
    f32 = jnp.float32
    bf16 = jnp.bfloat16
    n_rope = 6 if with_rope else 0
    (x_ref, g_ref, sc_ref, sh_ref, bg_ref, wz_ref, wx_ref, wq_ref, wc_ref, wg_ref, wm_ref,
     wuq_ref, wuk_ref, wuv_ref, gsq_ref, gsk_ref, gqa_ref, gkva_ref, gmq_ref, gmk_ref) = refs[:20]
    rope_refs = refs[20:20 + n_rope]
    (z_ref, xbc_ref, gate_ref, misc_ref, sq_ref, sk_ref, sv_ref, mq_ref, mk_ref, mv_ref) = refs[20 + n_rope:]
    swa_rope = tuple(r[...] for r in rope_refs[0:3]) if with_rope else None
    mla_rope = tuple(r[...] for r in rope_refs[3:6]) if with_rope else None

    h = (_rms(x_ref[...]) * g_ref[...] * (1.0 + sc_ref[0]) + sh_ref[0]).astype(bf16)
    z_ref[...] = jnp.dot(h, wz_ref[...], preferred_element_type=f32)
    xbc_ref[...] = jnp.dot(h, wx_ref[...], preferred_element_type=f32)
    gate_ref[...] = jax.nn.sigmoid(jnp.dot(h, wg_ref[...], preferred_element_type=f32) + bg_ref[...])
    misc = jnp.dot(h, wm_ref[...], preferred_element_type=f32)
    misc_ref[...] = misc

    qkv = jnp.dot(h, wq_ref[...], preferred_element_type=f32)
    slot = lambda a, i: a[:, i * LANES:(i + 1) * LANES]
    for hd in range(SWA_HEADS):
        y = _slot_norm_rope(slot(qkv, hd), SWA_HEAD_DIM, gsq_ref[...], swa_rope, SWA_HEAD_DIM // 2)
        sq_ref[0, hd] = y[:, :SWA_HEAD_DIM].astype(bf16)
    for hd in range(SWA_KV_HEADS):
        y = _slot_norm_rope(slot(qkv, SWA_HEADS + hd), SWA_HEAD_DIM, gsk_ref[...], swa_rope, SWA_HEAD_DIM // 2)
        sk_ref[0, hd] = y[:, :SWA_HEAD_DIM].astype(bf16)
        sv_ref[0, hd] = slot(qkv, SWA_HEADS + SWA_KV_HEADS + hd)[:, :SWA_HEAD_DIM].astype(bf16)

    cq = jnp.dot(h, wc_ref[...], preferred_element_type=f32)
    qa = (_rms(cq[:, :MLA_Q_RANK]) * gqa_ref[...]).astype(bf16)
    kva = (_rms(cq[:, MLA_Q_RANK:]) * gkva_ref[...]).astype(bf16)
    q_up = jnp.dot(qa, wuq_ref[...], preferred_element_type=f32)
    k_up = jnp.dot(kva, wuk_ref[...], preferred_element_type=f32)
    v_up = jnp.dot(kva, wuv_ref[...], preferred_element_type=f32)
    lane = lax.broadcasted_iota(jnp.int32, misc.shape, 1)
    k_rope = jnp.where((lane >= MISC_ROPE_AT) & (lane < MISC_ROPE_AT + MLA_ROPE), misc, 0.0)
    for hd in range(MLA_HEADS):
        mq_ref[0, hd] = _slot_norm_rope(slot(q_up, hd), MLA_QK, gmq_ref[...], mla_rope,
                                        MLA_ROPE // 2).astype(bf16)
        mk_ref[0, hd] = _slot_norm_rope(slot(k_up, hd) + k_rope, MLA_QK, gmk_ref[...], mla_rope,
                                        MLA_ROPE // 2).astype(bf16)
        mv_ref[0, hd] = slot(v_up, hd)[:, :MLA_V].astype(bf16)


def _slot_pack(w, n_heads, width):
    k = w.shape[0]
    w3 = w.reshape(k, n_heads, width)
    return jnp.pad(w3, ((0, 0), (0, 0), (0, LANES - width))).reshape(k, n_heads * LANES)


def _slot_row(v, width_to=LANES):
    v = v.astype(jnp.float32)
    return jnp.pad(v, (0, width_to - v.shape[0])).reshape(1, width_to)


def _pack_weights(p):
    w_in = p['w_in']
    pts = [0] + split_points() + [IN_WIDTH]
    col = lambda i: w_in[:, pts[i]:pts[i + 1]]
    bf16 = jnp.bfloat16
    zeros = jnp.zeros((w_in.shape[0], MISC_ROPE_AT), w_in.dtype)
    qkv = jnp.concatenate([_slot_pack(col(4), SWA_HEADS, SWA_HEAD_DIM),
                           _slot_pack(col(5), SWA_KV_HEADS, SWA_HEAD_DIM),
                           _slot_pack(col(6), SWA_KV_HEADS, SWA_HEAD_DIM)], axis=1)
    ukv = p['mla_w_ukv'].reshape(MLA_KV_RANK, MLA_HEADS, MLA_NOPE + MLA_V)
    weights = (col(0), col(1), qkv, jnp.concatenate([col(7), col(8)], axis=1),
               jnp.concatenate([col(10), col(11), col(12)], axis=1),
               jnp.concatenate([zeros, col(9), col(2), col(3)], axis=1),
               _slot_pack(p['mla_w_uq'], MLA_HEADS, MLA_QK),
               _slot_pack(ukv[:, :, :MLA_NOPE].reshape(MLA_KV_RANK, -1), MLA_HEADS, MLA_NOPE),
               _slot_pack(ukv[:, :, MLA_NOPE:].reshape(MLA_KV_RANK, -1), MLA_HEADS, MLA_V))
    gains = (_slot_row(p['swa_q_norm'] * SWA_SCALE), _slot_row(p['swa_k_norm']),
             _slot_row(p['mla_q_a_norm'], MLA_Q_RANK), _slot_row(p['mla_kv_a_norm'], MLA_KV_RANK),
             _slot_row(p['mla_q_norm'] * MLA_SCALE), _slot_row(p['mla_k_norm']))
    return tuple(w.astype(bf16) for w in weights), gains


def _rope_slot_tables(cos, sin, start):
    n, half = cos.shape
    ones = lambda w: jnp.ones((n, w), jnp.float32)
    zeros = lambda w: jnp.zeros((n, w), jnp.float32)
    tail = LANES - start - 2 * half
    c = jnp.concatenate([ones(start), cos, cos, ones(tail)], axis=1)
    s_lo = jnp.concatenate([zeros(start), -sin, zeros(half), zeros(tail)], axis=1)
    s_hi = jnp.concatenate([zeros(start), zeros(half), sin, zeros(tail)], axis=1)
    return c, s_lo, s_hi


def _batch_row_spec(width, tiles_per_batch):
    return pl.BlockSpec((1, 1, width), lambda t: (t // tiles_per_batch, 0, 0))


def premix(x, norm_g, scale, shift, b_gate, packed, rope_tables):
    bsz, t_len, d = x.shape
    n = bsz * t_len
    tm = min(PROJ_TM, t_len)
    tpb = t_len // tm
    f32 = jnp.float32
    bf16 = jnp.bfloat16
    weights, gains = packed
    const = lambda a: pl.BlockSpec(a.shape, lambda t: (0,) * a.ndim)
    tok = lambda w: pl.BlockSpec((tm, w), lambda t: (t, 0))
    heads = lambda nh, w: pl.BlockSpec((1, nh, tm, w), lambda t: (t // tpb, 0, t % tpb, 0))
    with_rope = rope_tables is not None
    rope_in = list(rope_tables) if with_rope else []
    rope_specs = [pl.BlockSpec((tm, LANES), lambda t: (t % tpb, 0)) for _ in rope_in]
    flat_w = (SSD_D_INNER, SSD_CONV_CH, 3 * D_MODEL, MISC_W)
    head_shapes = ((SWA_HEADS, SWA_HEAD_DIM), (SWA_KV_HEADS, SWA_HEAD_DIM), (SWA_KV_HEADS, SWA_HEAD_DIM),
                   (MLA_HEADS, LANES), (MLA_HEADS, LANES), (MLA_HEADS, MLA_V))
    small = [norm_g.reshape(1, d), scale, shift, b_gate.reshape(1, 3 * D_MODEL)]
    outs = pl.pallas_call(
        functools.partial(_premix_kernel, with_rope=with_rope),
        grid=(n // tm,),
        in_specs=([tok(d), const(small[0]), _batch_row_spec(d, tpb), _batch_row_spec(d, tpb), const(small[3])]
                  + [const(w) for w in weights] + [const(g) for g in gains] + rope_specs),
        out_specs=[tok(w) for w in flat_w] + [heads(nh, w) for nh, w in head_shapes],
        out_shape=([jax.ShapeDtypeStruct((n, w), f32) for w in flat_w]
                   + [jax.ShapeDtypeStruct((bsz, nh, t_len, w), bf16) for nh, w in head_shapes]),
        compiler_params=pltpu.CompilerParams(dimension_semantics=("parallel",),
                                             vmem_limit_bytes=PEER_VMEM_LIMIT),
        name="premix",
    )(x.reshape(n, d), *small, *weights, *gains, *rope_in)
    return [o.reshape(bsz, t_len, -1) for o in outs[:4]] + list(outs[4:])


def _heads_to_lanes(y_ref):
    return jnp.concatenate([y_ref[0, h].astype(jnp.bfloat16) for h in range(y_ref.shape[1])], axis=-1)


def _merge_kernel(x_ref, yf_ref, yb_ref, z_ref, yw_ref, ym_ref, gate_ref, gs_ref, g1_ref, n2_ref, sc_ref,
                  sh_ref, w1_ref, w2_ref, w3_ref, wo_ref, xo_ref, h2_ref):
    f32 = jnp.float32
    bf16 = jnp.bfloat16
    d = D_MODEL
    z = z_ref[...]
    ys = _rms((yf_ref[...] + yb_ref[...]) * (z * jax.nn.sigmoid(z))) * gs_ref[...]
    m = (gate_ref[:, 0:d] * jnp.dot(ys.astype(bf16), w1_ref[...], preferred_element_type=f32)
         + gate_ref[:, d:2 * d] * jnp.dot(_heads_to_lanes(yw_ref), w2_ref[...], preferred_element_type=f32)
         + gate_ref[:, 2 * d:3 * d] * jnp.dot(_heads_to_lanes(ym_ref), w3_ref[...], preferred_element_type=f32))
    mix = jnp.dot(m.astype(bf16), wo_ref[...], preferred_element_type=f32)
    xn = x_ref[...] + g1_ref[0] * mix
    xo_ref[...] = xn
    h2_ref[...] = (_rms(xn) * n2_ref[...] * (1.0 + sc_ref[0]) + sh_ref[0]).astype(bf16)


def merge_norm(x, yf, yb, y_block, z, yw, ym, gates, p, g1, sc2, sh2):
    bsz, t_len, d = x.shape
    n = bsz * t_len
    tm = min(PROJ_TM, t_len)
    bf16 = jnp.bfloat16
    const = lambda shape: pl.BlockSpec(shape, lambda t: (0,) * len(shape))
    tok = lambda w: pl.BlockSpec((tm, w), lambda t: (t, 0))
    brow = _batch_row_spec(d, t_len // tm)
    ytok = pl.BlockSpec((tm, d), lambda t: (y_block(t, tm), 0))
    tpb = t_len // tm
    heads = lambda a: pl.BlockSpec((1, a.shape[1], tm, a.shape[3]), lambda t: (t // tpb, 0, t % tpb, 0))
    flat = lambda a: a.reshape(-1, a.shape[-1])
    swa_w = SWA_HEADS * SWA_HEAD_DIM
    mla_w = MLA_HEADS * MLA_V
    xo, h2 = pl.pallas_call(
        _merge_kernel,
        grid=(n // tm,),
        in_specs=[tok(d), ytok, ytok, tok(d), heads(yw), heads(ym), tok(3 * d),
                  const((1, d)), brow, const((1, d)), brow, brow,
                  const((d, d)), const((swa_w, d)), const((mla_w, d)), const((d, d))],
        out_specs=[tok(d), tok(d)],
        out_shape=[jax.ShapeDtypeStruct((n, d), jnp.float32), jax.ShapeDtypeStruct((n, d), bf16)],
        compiler_params=pltpu.CompilerParams(dimension_semantics=("parallel",),
                                             vmem_limit_bytes=PEER_VMEM_LIMIT),
        name="merge_norm",
    )(flat(x), flat(yf), flat(yb), flat(z), yw, ym, flat(gates),
      p['ssd_norm_g'].reshape(1, d), g1, p['norm2_g'].reshape(1, d), sc2, sh2,
      p['w_br_ssd'].astype(bf16), p['w_br_swa'].astype(bf16), p['w_br_mla'].astype(bf16),
      p['w_out'].astype(bf16))
    return xo.reshape(bsz, t_len, d), h2.reshape(bsz, t_len, d)


def split_points():
    pts, acc = [], 0
    for s in IN_SIZES[:-1]:
        acc += s
        pts.append(acc)
    return pts


def rms_norm(x, g):
    xf = x.astype(jnp.float32)
    y = xf * lax.rsqrt(jnp.mean(xf * xf, axis=-1, keepdims=True) + EPS)
    return (y * g.astype(jnp.float32)).astype(x.dtype)


def axial_rope(rows, dim):
    pairs = dim // 4
    freqs = ROPE_BASE ** (-jnp.arange(pairs, dtype=jnp.float32) / pairs)
    pos_r = jnp.repeat(jnp.arange(rows, dtype=jnp.float32), GRID_W)
    pos_c = jnp.tile(jnp.arange(GRID_W, dtype=jnp.float32), rows)
    ang = jnp.concatenate([pos_r[:, None] * freqs, pos_c[:, None] * freqs], axis=-1)
    return jnp.cos(ang), jnp.sin(ang)


def apply_rope(x, cos, sin):
    d2 = x.shape[-1] // 2
    x1, x2 = x[..., :d2], x[..., d2:]
    cs, sn = cos[:, None, :], sin[:, None, :]
    return jnp.concatenate([x1 * cs - x2 * sn, x2 * cs + x1 * sn], axis=-1).astype(x.dtype)


def dwconv_centred(u, w, b):
    pad = (SSD_CONV - 1) // 2
    y = lax.conv_general_dilated(u, w[:, None, :].astype(u.dtype), window_strides=(1,),
                                 padding=[(pad, pad)], dimension_numbers=('NWC', 'WIO', 'NWC'),
                                 feature_group_count=u.shape[-1])
    return y + b


def ssd_mixer(lat, ctx, p, with_ctx_out):
    f32 = jnp.float32

    def conv_act(xbc):
        return jax.nn.silu(dwconv_centred(xbc, p['ssd_conv_w'], p['ssd_conv_b']))

    n_ctx = ctx[1].shape[1]
    u_all = jnp.concatenate([conv_act(ctx[1]), conv_act(lat[1])], axis=1)
    ys = []
    for direction in range(2):
        dt_raw = jnp.concatenate([ctx[2 + direction], lat[2 + direction]], axis=1)
        ys.append(ssd_scan_dir(u_all, dt_raw, p['ssd_dt_bias'][direction], p['ssd_a_log'][direction],
                               p['ssd_d'][direction], direction == 1, n_ctx))
    return ys


def swa_mixer(lat, ctx, p, with_ctx_out):
    ql, kl, vl = lat
    qc, kc, vc = ctx
    y_l = swa_attention(ql, kl, vl, kc, vc, p['swa_sink'])
    y_c = None
    if with_ctx_out:
        n_ctx = qc.shape[2]
        y_c = flash_attention(qc, kc, vc, n_ctx, n_ctx, sink=p['swa_sink'])
    return y_l, y_c


def mla_mixer(lat, ctx, with_ctx_out):
    ql, kl, vl = lat
    qc, kc, vc = ctx
    k_all = jnp.concatenate([kc, kl], axis=2)
    v_all = jnp.concatenate([vc, vl], axis=2)
    y_l = flash_attention(ql, k_all, v_all, MLA_TQ, MLA_TK)
    y_c = None
    if with_ctx_out:
        n_ctx = qc.shape[2]
        y_c = flash_attention(qc, kc, vc, n_ctx, n_ctx)
    return y_l, y_c


PEER_ROUTE_TM = 256
PEER_TM = 512
PEER_EC = 512
PEER_VMEM_LIMIT = 56 * 1024 * 1024
_SQRT_HALF = 0.7071067811865476


def _top16_rows(s):
    rows = []
    for _ in range(PEER_TOPK):
        m = jnp.max(s, axis=0, keepdims=True)
        rows.append(m)
        s = jnp.where(s == m, -jnp.inf, s)
    return jnp.concatenate(rows, axis=0)


def _peer_route_kernel(h_ref, wq_ref, keys_ref, s0_ref, s1_ref, a_ref, b_ref, tau_ref, q_scr):
    f32 = jnp.float32
    q = jnp.dot(h_ref[...].astype(jnp.bfloat16), wq_ref[...], preferred_element_type=f32)
    q_scr[...] = q.astype(jnp.bfloat16)

    def head_body(hd, carry):
        def scores(p):
            col = pl.multiple_of((hd * 2 + p) * PEER_HALF, PEER_HALF)
            qs = q_scr[:, pl.ds(col, PEER_HALF)]
            return lax.dot_general(keys_ref[p * PEER_HEADS + hd], qs, (((1,), (1,)), ((), ())),
                                   preferred_element_type=f32)

        s0 = scores(0)
        s1 = scores(1)
        a_top = _top16_rows(s0)
        b_top = _top16_rows(s1)
        cand = [a_top[0:1] + b_top]
        cand += [a_top[r:r + 1] + b_top[0:8] for r in range(1, 8)]
        cand.append(a_top[8:16] + b_top[0:1])
        best = _top16_rows(jnp.concatenate(cand, axis=0))
        z = jnp.sum(jnp.exp(best - best[0:1]), axis=0, keepdims=True)
        a = jnp.exp(s0 - a_top[0:1])
        b = jnp.exp(s1 - b_top[0:1]) / z
        tau = best[PEER_TOPK - 1:PEER_TOPK]
        for tl in range(s0.shape[1] // LANES):
            ts = slice(tl * LANES, (tl + 1) * LANES)
            s0_ref[hd, tl] = s0[:, ts]
            s1_ref[hd, tl] = s1[:, ts]
            a_ref[hd, tl] = a[:, ts]
            b_ref[hd, tl] = b[:, ts]
            tau_ref[hd, tl] = tau[:, ts]
        return carry

    lax.fori_loop(0, PEER_HEADS, head_body, 0, unroll=2)


def _gelu_exact(x):
    return 0.5 * x * (1.0 + lax.erf(x * _SQRT_HALF))


def _peer_expert_kernel(h_ref, u_ref, vt_ref, s0_ref, s1_ref, a_ref, b_ref, tau_ref, x_ref, g2_ref,
                        o_ref, acc_ref, act_scr, g_scr):
    f32 = jnp.float32
    j = pl.program_id(1)
    rows_per_step = PEER_EC // PEER_N_KEYS

    @pl.when(j == 0)
    def _():
        acc_ref[...] = jnp.zeros_like(acc_ref)

    act_scr[...] = lax.dot_general(u_ref[...], h_ref[...], (((1,), (1,)), ((), ())),
                                   preferred_element_type=f32)
    n_tl = PEER_TM // LANES

    def tile_body(t, carry):
        ii = t // n_tl
        tl = t % n_tl
        i = j * rows_per_step + ii
        w = jnp.zeros((PEER_N_KEYS, LANES), f32)
        for hd in range(PEER_HEADS):
            s0_row = s0_ref[hd, tl, pl.ds(i, 1), :]
            a_row = a_ref[hd, tl, pl.ds(i, 1), :]
            keep = (s1_ref[hd, tl] + s0_row) >= tau_ref[hd, tl]
            w = w + jnp.where(keep, b_ref[hd, tl], 0.0) * a_row
        es = pl.ds(pl.multiple_of(ii * PEER_N_KEYS, PEER_N_KEYS), PEER_N_KEYS)
        ts = pl.ds(pl.multiple_of(tl * LANES, LANES), LANES)
        g_scr[es, ts] = (w * _gelu_exact(act_scr[es, ts])).astype(jnp.bfloat16)
        return carry

    lax.fori_loop(0, rows_per_step * n_tl, tile_body, 0, unroll=4)
    acc_ref[...] += jnp.dot(vt_ref[...], g_scr[...], preferred_element_type=f32)

    @pl.when(j == pl.num_programs(1) - 1)
    def _():
        o_ref[...] = x_ref[...] + g2_ref[0] * acc_ref[...].T


def peer_ffn(h, x, g2_tiles, w_q, sub_keys, u_tab, v_tab):
    bsz, T, D = h.shape
    n_tok = bsz * T
    n_exp = u_tab.shape[0]
    hf = h.reshape(n_tok, D)
    f32 = jnp.float32
    bf16 = jnp.bfloat16
    keys = sub_keys.reshape(2 * PEER_HEADS, PEER_N_KEYS, PEER_HALF).astype(bf16)
    tm = PEER_ROUTE_TM
    n_tiles = n_tok // LANES
    big = jax.ShapeDtypeStruct((PEER_HEADS, n_tiles, PEER_N_KEYS, LANES), f32)
    small = jax.ShapeDtypeStruct((PEER_HEADS, n_tiles, 1, LANES), f32)
    big_spec = pl.BlockSpec((PEER_HEADS, tm // LANES, PEER_N_KEYS, LANES), lambda t: (0, t, 0, 0))
    small_spec = pl.BlockSpec((PEER_HEADS, tm // LANES, 1, LANES), lambda t: (0, t, 0, 0))
    s0, s1, a, b, tau = pl.pallas_call(
        _peer_route_kernel,
        grid=(n_tok // tm,),
        in_specs=[pl.BlockSpec((tm, D), lambda t: (t, 0)),
                  pl.BlockSpec((D, PEER_HEADS * PEER_D_KEY), lambda t: (0, 0)),
                  pl.BlockSpec((2 * PEER_HEADS, PEER_N_KEYS, PEER_HALF), lambda t: (0, 0, 0))],
        out_specs=[big_spec, big_spec, big_spec, big_spec, small_spec],
        out_shape=[big, big, big, big, small],
        scratch_shapes=[pltpu.VMEM((tm, PEER_HEADS * PEER_D_KEY), bf16)],
        compiler_params=pltpu.CompilerParams(dimension_semantics=("parallel",),
                                             vmem_limit_bytes=PEER_VMEM_LIMIT),
        name="peer_route",
    )(hf, w_q.astype(bf16), keys)

    tm = PEER_TM
    ec = PEER_EC
    big_spec = pl.BlockSpec((PEER_HEADS, tm // LANES, PEER_N_KEYS, LANES), lambda t, j: (0, t, 0, 0))
    small_spec = pl.BlockSpec((PEER_HEADS, tm // LANES, 1, LANES), lambda t, j: (0, t, 0, 0))
    out = pl.pallas_call(
        _peer_expert_kernel,
        grid=(n_tok // tm, n_exp // ec),
        in_specs=[pl.BlockSpec((tm, D), lambda t, j: (t, 0)),
                  pl.BlockSpec((ec, D), lambda t, j: (j, 0)),
                  pl.BlockSpec((D, ec), lambda t, j: (0, j)),
                  big_spec, big_spec, big_spec, big_spec, small_spec,
                  pl.BlockSpec((tm, D), lambda t, j: (t, 0)),
                  pl.BlockSpec((1, 1, D), lambda t, j: (t, 0, 0))],
        out_specs=pl.BlockSpec((tm, D), lambda t, j: (t, 0)),
        out_shape=jax.ShapeDtypeStruct((n_tok, D), f32),
        scratch_shapes=[pltpu.VMEM((D, tm), f32), pltpu.VMEM((ec, tm), f32), pltpu.VMEM((ec, tm), bf16)],
        compiler_params=pltpu.CompilerParams(dimension_semantics=("parallel", "arbitrary"),
                                             vmem_limit_bytes=PEER_VMEM_LIMIT),
        name="peer_expert",
    )(hf, u_tab.astype(bf16), v_tab.T.astype(bf16), s0, s1, a, b, tau, x.reshape(n_tok, D), g2_tiles)
    return out.reshape(bsz, T, D)


def _split_cols(t, sizes):
    out, acc = [], 0
    for s in sizes:
        out.append(t[..., acc:acc + s])
        acc += s
    return out


def layer(xl, xc, c, c_ctx, p, rope_tables, with_ctx_out):
    bsz, seq, d = xl.shape
    n_ctx = xc.shape[1]
    mod_l = (jax.nn.silu(c) @ p['w_mod'] + p['b_mod'])[:, None, :]
    mod_c = jnp.broadcast_to((jax.nn.silu(c_ctx) @ p['w_mod'] + p['b_mod'])[None, None, :], mod_l.shape)
    sh1_l, sc1_l, g1_l, sh2_l, sc2_l, g2_l = jnp.split(mod_l, 6, axis=-1)
    sh1_c, sc1_c, g1_c, sh2_c, sc2_c, g2_c = jnp.split(mod_c, 6, axis=-1)

    packed = _pack_weights(p)
    dt_at = MISC_ROPE_AT + MLA_ROPE

    def project(x, sc, sh, rope):
        z, xbc, gates, misc, sq, sk, sv, mq, mk, mv = premix(x, p['norm1_g'], sc, sh, p['b_gate'], packed, rope)
        dtf = misc[..., dt_at:dt_at + SSD_HEADS]
        dtb = misc[..., dt_at + SSD_HEADS:dt_at + 2 * SSD_HEADS]
        return dict(z=z, gates=gates, ssd=(None, xbc, dtf, dtb), swa=(sq, sk, sv), mla=(mq, mk, mv))

    pl_ = project(xl, sc1_l, sh1_l, rope_tables)
    pc = project(xc, sc1_c, sh1_c, None)
    yf, yb = ssd_mixer(pl_['ssd'], pc['ssd'], p, with_ctx_out)
    y_swa_l, y_swa_c = swa_mixer(pl_['swa'], pc['swa'], p, with_ctx_out)
    y_mla_l, y_mla_c = mla_mixer(pl_['mla'], pc['mla'], with_ctx_out)

    def lat_block(t, tm):
        return (t // (seq // tm)) * ((n_ctx + seq) // tm) + n_ctx // tm + t % (seq // tm)

    def ctx_block(t, tm):
        return (t // (n_ctx // tm)) * ((n_ctx + seq) // tm) + t % (n_ctx // tm)

    peer = (p['peer_w_q'], p['peer_keys'], p['peer_u'], p['peer_v'])
    xl, h2_l = merge_norm(xl, yf, yb, lat_block, pl_['z'], y_swa_l, y_mla_l, pl_['gates'], p,
                          g1_l, sc2_l, sh2_l)
    if with_ctx_out:
        xc, h2_c = merge_norm(xc, yf, yb, ctx_block, pc['z'], y_swa_c, y_mla_c, pc['gates'], p,
                              g1_c, sc2_c, sh2_c)
        g2_c_tiles = jnp.broadcast_to(g2_c[:1], (bsz * n_ctx // PEER_TM, 1, d))
        xc = peer_ffn(h2_c, xc, g2_c_tiles, *peer)
    xl = peer_ffn(h2_l, xl, jnp.repeat(g2_l, seq // PEER_TM, axis=0), *peer)
    return xl, xc


def kernel(x, c, ctx, c_ctx, w_mod, b_mod, norm1_g, norm2_g, w_in, ssd_conv_w, ssd_conv_b, ssd_dt_bias,
           ssd_a_log, ssd_d, ssd_norm_g, swa_q_norm, swa_k_norm, swa_sink, mla_q_a_norm, mla_kv_a_norm,
           mla_w_uq, mla_w_ukv, mla_q_norm, mla_k_norm, b_gate, w_br_ssd, w_br_swa, w_br_mla, w_out,
           peer_w_q, peer_keys, peer_u, peer_v):
    L = x.shape[1]
    rows = L // GRID_W
    rope_tables = (_rope_slot_tables(*axial_rope(rows, SWA_HEAD_DIM), 0)
                   + _rope_slot_tables(*axial_rope(rows, MLA_ROPE), MLA_NOPE))
    params = dict(w_mod=w_mod, b_mod=b_mod, norm1_g=norm1_g, norm2_g=norm2_g, w_in=w_in,
                  ssd_conv_w=ssd_conv_w, ssd_conv_b=ssd_conv_b, ssd_dt_bias=ssd_dt_bias,
                  ssd_a_log=ssd_a_log, ssd_d=ssd_d, ssd_norm_g=ssd_norm_g, swa_q_norm=swa_q_norm,
                  swa_k_norm=swa_k_norm, swa_sink=swa_sink, mla_q_a_norm=mla_q_a_norm,
                  mla_kv_a_norm=mla_kv_a_norm, mla_w_uq=mla_w_uq, mla_w_ukv=mla_w_ukv,
                  mla_q_norm=mla_q_norm, mla_k_norm=mla_k_norm, b_gate=b_gate, w_br_ssd=w_br_ssd,
                  w_br_swa=w_br_swa, w_br_mla=w_br_mla, w_out=w_out, peer_w_q=peer_w_q,
                  peer_keys=peer_keys, peer_u=peer_u, peer_v=peer_v)
    xl, xc = x, ctx
    for i in range(DEPTH):
        p = {k: v[i] for k, v in params.items()}
        xl, xc = layer(xl, xc, c, c_ctx, p, rope_tables, i < DEPTH - 1)
    return xl
```

```python
import functools

import jax
import jax.numpy as jnp
from jax import lax
from jax.experimental import pallas as pl
from jax.experimental.pallas import tpu as pltpu

D_MODEL = 1024
DEPTH = 2
GRID_W = 64
EPS = 1e-6
NEG_INF = -1e30
ROPE_BASE = 10000.0

SSD_D_INNER = 1024
SSD_HEAD_DIM = 64
SSD_HEADS = 16
SSD_GROUPS = 2
SSD_HEADS_PER_GROUP = 8
SSD_STATE = 64
SSD_CONV = 5
SSD_CHUNK = 128
SSD_CONV_CH = 1280

SWA_HEADS = 8
SWA_KV_HEADS = 2
SWA_HEAD_DIM = 64
SWA_WINDOW = 128
SWA_BLOCK = 128
SWA_SCALE = SWA_HEAD_DIM ** -0.5

MLA_HEADS = 8
MLA_Q_RANK = 384
MLA_KV_RANK = 256
MLA_NOPE = 64
MLA_ROPE = 32
MLA_V = 64
MLA_QK = MLA_NOPE + MLA_ROPE
MLA_BLOCK = 128
MLA_SCALE = MLA_QK ** -0.5
MLA_TQ = 1024
MLA_TK = 768

PEER_HEADS = 8
PEER_N_KEYS = 128
PEER_D_KEY = 256
PEER_HALF = 128
PEER_TOPK = 16
PEER_BLOCK = 64

IN_SIZES = (SSD_D_INNER, SSD_CONV_CH, SSD_HEADS, SSD_HEADS,
            SWA_HEADS * SWA_HEAD_DIM, SWA_KV_HEADS * SWA_HEAD_DIM, SWA_KV_HEADS * SWA_HEAD_DIM,
            MLA_Q_RANK, MLA_KV_RANK, MLA_ROPE,
            D_MODEL, D_MODEL, D_MODEL)
IN_WIDTH = sum(IN_SIZES)

LANES = 128


def _round_up(n, m):
    return (n + m - 1) // m * m


def _mm_kernel(a_ref, b_ref, o_ref):
    o_ref[...] = jnp.dot(a_ref[...].astype(jnp.bfloat16), b_ref[...],
                         preferred_element_type=jnp.float32)


def _pick_tile(n, cap, unit):
    t = min(n, cap)
    while n % t or t % unit:
        t -= unit
    return t


def pmatmul(a, w):
    m, k = a.shape
    n = w.shape[1]
    n_pad = _round_up(n, LANES)
    wb = w.astype(jnp.bfloat16)
    if n_pad != n:
        wb = jnp.pad(wb, ((0, 0), (0, n_pad - n)))
    tm = _pick_tile(m, 512, 8)
    tn = _pick_tile(n_pad, 1024, LANES)
    out = pl.pallas_call(
        _mm_kernel,
        grid=(m // tm, n_pad // tn),
        in_specs=[pl.BlockSpec((tm, k), lambda i, j: (i, 0)),
                  pl.BlockSpec((k, tn), lambda i, j: (0, j))],
        out_specs=pl.BlockSpec((tm, tn), lambda i, j: (i, j)),
        out_shape=jax.ShapeDtypeStruct((m, n_pad), jnp.float32),
        compiler_params=pltpu.CompilerParams(dimension_semantics=("parallel", "parallel")),
        name="matmul",
    )(a, wb)
    return out[:, :n] if n_pad != n else out


def mm(a, w):
    lead = a.shape[:-1]
    return pmatmul(a.reshape(-1, a.shape[-1]), w).reshape(lead + (w.shape[1],))


def _flash_kernel(*refs, tk, has_sink):
    f32 = jnp.float32
    if has_sink:
        q_ref, k_ref, v_ref, sink_ref, o_ref = refs
    else:
        q_ref, k_ref, v_ref, o_ref = refs
    q = q_ref[0, 0]
    tq = q.shape[0]
    dv = v_ref.shape[-1]
    n_chunks = k_ref.shape[2] // tk

    def body(c, carry):
        m, l, acc = carry
        off = pl.multiple_of(c * tk, tk)
        k = k_ref[0, 0, pl.ds(off, tk), :]
        v = v_ref[0, 0, pl.ds(off, tk), :]
        s = lax.dot_general(q, k, (((1,), (1,)), ((), ())), preferred_element_type=f32)
        m_new = jnp.maximum(m, jnp.max(s, axis=-1, keepdims=True))
        p = jnp.exp(s - m_new)
        alpha = jnp.exp(m - m_new)
        l = alpha * l + jnp.sum(p, axis=-1, keepdims=True)
        acc = alpha * acc + jnp.dot(p.astype(jnp.bfloat16), v, preferred_element_type=f32)
        return m_new, l, acc

    if has_sink:
        init = (jnp.broadcast_to(sink_ref[0], (tq, 1)), jnp.ones((tq, 1), f32), jnp.zeros((tq, dv), f32))
    else:
        init = (jnp.full((tq, 1), -jnp.inf, f32), jnp.zeros((tq, 1), f32), jnp.zeros((tq, dv), f32))
    _, l, acc = lax.fori_loop(0, n_chunks, body, init, unroll=True)
    o_ref[0, 0] = acc / l


def flash_attention(q, k, v, tq, tk, sink=None):
    bsz, nh, lq, dq = q.shape
    nk, dv = v.shape[2], v.shape[3]
    grp = nh // k.shape[1]
    bf16 = jnp.bfloat16
    in_specs = [pl.BlockSpec((1, 1, tq, dq), lambda b, h, i: (b, h, i, 0)),
                pl.BlockSpec((1, 1, nk, dq), lambda b, h, i: (b, h // grp, 0, 0)),
                pl.BlockSpec((1, 1, nk, dv), lambda b, h, i: (b, h // grp, 0, 0))]
    args = [q.astype(bf16), k.astype(bf16), v.astype(bf16)]
    if sink is not None:
        in_specs.append(pl.BlockSpec((1, 1, 1), lambda b, h, i: (h, 0, 0)))
        args.append(sink.astype(jnp.float32).reshape(nh, 1, 1))
    return pl.pallas_call(
        functools.partial(_flash_kernel, tk=tk, has_sink=sink is not None),
        grid=(bsz, nh, lq // tq),
        in_specs=in_specs,
        out_specs=pl.BlockSpec((1, 1, tq, dv), lambda b, h, i: (b, h, i, 0)),
        out_shape=jax.ShapeDtypeStruct((bsz, nh, lq, dv), jnp.float32),
        compiler_params=pltpu.CompilerParams(dimension_semantics=("parallel", "parallel", "parallel"),
                                             vmem_limit_bytes=PEER_VMEM_LIMIT),
        name="flash_attention",
    )(*args)


SWA_GROUP = SWA_HEADS // SWA_KV_HEADS
SWA_BAND = 3 * SWA_BLOCK
SWA_STEP_BLOCKS = 4


def _swa_kernel(q_ref, k_ref, v_ref, kc_ref, vc_ref, sink_ref, o_ref):
    for sb in range(SWA_STEP_BLOCKS):
        _swa_block(pl.program_id(2) * SWA_STEP_BLOCKS + sb, slice(sb * SWA_BLOCK, (sb + 1) * SWA_BLOCK),
                   q_ref, k_ref, v_ref, kc_ref, vc_ref, sink_ref, o_ref)


def _swa_block(n, qs, q_ref, k_ref, v_ref, kc_ref, vc_ref, sink_ref, o_ref):
    f32 = jnp.float32
    bf16 = jnp.bfloat16
    seq = k_ref.shape[2]
    rows = SWA_GROUP * SWA_BLOCK
    q = q_ref[0, :, qs, :].reshape(rows, SWA_HEAD_DIM)
    start = pl.multiple_of(jnp.clip((n - 1) * SWA_BLOCK, 0, seq - SWA_BAND), SWA_BLOCK)
    kb = k_ref[0, 0, pl.ds(start, SWA_BAND), :]
    vb = v_ref[0, 0, pl.ds(start, SWA_BAND), :]
    nt = (((1,), (1,)), ((), ()))
    s_band = lax.dot_general(q, kb, nt, preferred_element_type=f32)
    s_ctx = lax.dot_general(q, kc_ref[0, 0], nt, preferred_element_type=f32)
    qpos = n * SWA_BLOCK + lax.broadcasted_iota(jnp.int32, (rows, SWA_BAND), 0) % SWA_BLOCK
    kpos = start + lax.broadcasted_iota(jnp.int32, (rows, SWA_BAND), 1)
    s_band = jnp.where(jnp.abs(kpos - qpos) <= SWA_WINDOW, s_band, NEG_INF)
    sink = sink_ref[0]
    m = jnp.maximum(jnp.maximum(jnp.max(s_band, axis=-1, keepdims=True),
                                jnp.max(s_ctx, axis=-1, keepdims=True)), sink)
    p_band = jnp.exp(s_band - m)
    p_ctx = jnp.exp(s_ctx - m)
    denom = (jnp.sum(p_band, axis=-1, keepdims=True) + jnp.sum(p_ctx, axis=-1, keepdims=True)
             + jnp.exp(sink - m))
    o = (jnp.dot(p_band.astype(bf16), vb, preferred_element_type=f32)
         + jnp.dot(p_ctx.astype(bf16), vc_ref[0, 0], preferred_element_type=f32))
    o_ref[0, :, qs, :] = (o / denom).reshape(SWA_GROUP, SWA_BLOCK, SWA_HEAD_DIM)


def swa_attention(q, k, v, kc, vc, sink):
    bsz, _, seq, hd = q.shape
    n_ctx = kc.shape[2]
    bf16 = jnp.bfloat16
    sink_rows = jnp.repeat(sink.astype(jnp.float32), SWA_BLOCK).reshape(SWA_KV_HEADS, SWA_GROUP * SWA_BLOCK, 1)
    kv_spec = pl.BlockSpec((1, 1, seq, hd), lambda b, h, n: (b, h, 0, 0))
    ctx_spec = pl.BlockSpec((1, 1, n_ctx, hd), lambda b, h, n: (b, h, 0, 0))
    step_rows = SWA_STEP_BLOCKS * SWA_BLOCK
    return pl.pallas_call(
        _swa_kernel,
        grid=(bsz, SWA_KV_HEADS, seq // step_rows),
        in_specs=[pl.BlockSpec((1, SWA_GROUP, step_rows, hd), lambda b, h, n: (b, h, n, 0)),
                  kv_spec, kv_spec, ctx_spec, ctx_spec,
                  pl.BlockSpec((1, SWA_GROUP * SWA_BLOCK, 1), lambda b, h, n: (h, 0, 0))],
        out_specs=pl.BlockSpec((1, SWA_GROUP, step_rows, hd), lambda b, h, n: (b, h, n, 0)),
        out_shape=jax.ShapeDtypeStruct((bsz, SWA_HEADS, seq, hd), jnp.float32),
        compiler_params=pltpu.CompilerParams(dimension_semantics=("parallel", "parallel", "parallel")),
        name="swa_attention",
    )(q.astype(bf16), k.astype(bf16), v.astype(bf16), kc.astype(bf16), vc.astype(bf16), sink_rows)


SSD_PAIR = 2 * SSD_HEAD_DIM


def _softplus(x):
    return jnp.maximum(x, 0.0) + jnp.log1p(jnp.exp(-jnp.abs(x)))


def _ssd_kernel(u_ref, dt_ref, dtt_ref, bias_r_ref, bias_c_ref, alog_r_ref, alog_c_ref, dx_ref,
                y_ref, state_ref, *, reverse):
    f32 = jnp.float32
    bf16 = jnp.bfloat16
    q = SSD_CHUNK
    hi = lax.Precision.HIGHEST

    @pl.when(pl.program_id(1) == 0)
    def _():
        state_ref[...] = jnp.zeros_like(state_ref)

    row = lax.broadcasted_iota(jnp.int32, (q, q), 0)
    col = lax.broadcasted_iota(jnp.int32, (q, q), 1)
    before = (col >= row) if reverse else (col <= row)
    tri = before.astype(f32)

    dt = _softplus(dt_ref[0] + bias_r_ref[...])
    dtt = _softplus(dtt_ref[0] + bias_c_ref[...])
    a_r = -jnp.exp(alog_r_ref[...])
    a_c = -jnp.exp(alog_c_ref[...])
    cum = jnp.dot(tri, dt * a_r, precision=hi, preferred_element_type=f32)
    cumt = lax.dot_general(dtt * a_c, tri, (((1,), (1,)), ((), ())), precision=hi,
                           preferred_element_type=f32)
    tot = cum[0:1, :] if reverse else cum[q - 1:q, :]
    ecum = jnp.exp(cum)
    wgt = jnp.exp(tot - cum) * dt
    etot = jnp.exp(tot)

    lane_lo = lax.broadcasted_iota(jnp.int32, (q, SSD_PAIR), 1) < SSD_HEAD_DIM
    lane_lo_row = lane_lo[0:1, :]
    heads_per_group = SSD_HEADS_PER_GROUP
    for g in range(SSD_GROUPS):
        b_off = SSD_D_INNER + g * SSD_STATE
        c_off = SSD_D_INNER + SSD_GROUPS * SSD_STATE + g * SSD_STATE
        bg = u_ref[0, :, b_off:b_off + SSD_STATE].astype(bf16)
        cg = u_ref[0, :, c_off:c_off + SSD_STATE].astype(bf16)
        cb = lax.dot_general(cg, bg, (((1,), (1,)), ((), ())), preferred_element_type=f32)
        for kk in range(heads_per_group // 2):
            k = g * (heads_per_group // 2) + kk
            lanes = slice(k * SSD_PAIR, (k + 1) * SSD_PAIR)
            xp = u_ref[0, :, lanes]
            y = dx_ref[:, lanes] * xp
            for half in range(2):
                hh = 2 * k + half
                seg = cum[:, hh:hh + 1] - cumt[hh:hh + 1, :]
                decay = jnp.where(before, jnp.exp(seg), 0.0) * dtt[hh:hh + 1, :]
                m = (cb * decay).astype(bf16)
                xm = jnp.where(lane_lo if half == 0 else jnp.logical_not(lane_lo), xp, 0.0).astype(bf16)
                y = y + jnp.dot(m, xm, preferred_element_type=f32)
            e_pair = jnp.where(lane_lo, ecum[:, 2 * k:2 * k + 1], ecum[:, 2 * k + 1:2 * k + 2])
            w_pair = jnp.where(lane_lo, wgt[:, 2 * k:2 * k + 1], wgt[:, 2 * k + 1:2 * k + 2])
            t_pair = jnp.where(lane_lo_row, etot[:, 2 * k:2 * k + 1], etot[:, 2 * k + 1:2 * k + 2])
            prev = state_ref[:, lanes]
            y = y + jnp.dot(cg, prev.astype(bf16), preferred_element_type=f32) * e_pair
            y_ref[0, :, lanes] = y
            xw = (xp * w_pair).astype(bf16)
            s_new = lax.dot_general(bg, xw, (((0,), (0,)), ((), ())), preferred_element_type=f32)
            state_ref[:, lanes] = prev * t_pair + s_new


def ssd_scan_dir(u, dt_raw, bias, a_log, d_skip, reverse, n_ctx):
    bsz, t_all, _ = u.shape
    q = SSD_CHUNK
    nc = t_all // q
    ncc = n_ctx // q
    if reverse:
        def chunk(c):
            return jnp.where(c < ncc, ncc - 1 - c, nc + ncc - 1 - c)
    else:
        def chunk(c):
            return c
    f32 = jnp.float32
    row = lambda v: v.astype(f32).reshape(1, SSD_HEADS)
    colv = lambda v: v.astype(f32).reshape(SSD_HEADS, 1)
    small_r = pl.BlockSpec((1, SSD_HEADS), lambda b, c: (0, 0))
    small_c = pl.BlockSpec((SSD_HEADS, 1), lambda b, c: (0, 0))
    return pl.pallas_call(
        functools.partial(_ssd_kernel, reverse=reverse),
        grid=(bsz, nc),
        in_specs=[pl.BlockSpec((1, q, SSD_CONV_CH), lambda b, c: (b, chunk(c), 0)),
                  pl.BlockSpec((1, q, SSD_HEADS), lambda b, c: (b, chunk(c), 0)),
                  pl.BlockSpec((1, SSD_HEADS, q), lambda b, c: (b, 0, chunk(c))),
                  small_r, small_c, small_r, small_c,
                  pl.BlockSpec((1, SSD_D_INNER), lambda b, c: (0, 0))],
        out_specs=pl.BlockSpec((1, q, SSD_D_INNER), lambda b, c: (b, chunk(c), 0)),
        out_shape=jax.ShapeDtypeStruct((bsz, t_all, SSD_D_INNER), f32),
        scratch_shapes=[pltpu.VMEM((SSD_STATE, SSD_D_INNER), f32)],
        compiler_params=pltpu.CompilerParams(dimension_semantics=("parallel", "arbitrary")),
        name="ssd_scan_bwd" if reverse else "ssd_scan_fwd",
    )(u, dt_raw, jnp.swapaxes(dt_raw, 1, 2), row(bias), colv(bias), row(a_log), colv(a_log),
      jnp.repeat(d_skip.astype(f32), SSD_HEAD_DIM).reshape(1, SSD_D_INNER))


PROJ_TM = 256
SWA_SLOTS = SWA_HEADS + 2 * SWA_KV_HEADS
QKV_W = SWA_SLOTS * LANES
CQ_W = MLA_Q_RANK + MLA_KV_RANK
MISC_W = LANES
MISC_ROPE_AT = MLA_NOPE


def _rms(x):
    return x * lax.rsqrt(jnp.mean(x * x, axis=-1, keepdims=True) + EPS)


def _slot_norm_rope(x, width, gain, rope, half):
    y = x * lax.rsqrt(jnp.sum(x * x, axis=-1, keepdims=True) * (1.0 / width) + EPS) * gain
    if rope is None:
        return y
    c, s_lo, s_hi = rope
    return y * c + pltpu.roll(y, LANES - half, 1) * s_lo + pltpu.roll(y, half, 1) * s_hi


def _premix_kernel(*refs, with_rope):
    f32 = jnp.float32
    bf16 = jnp.bfloat16
    n_rope = 6 if with_rope else 0
    (x_ref, g_ref, sc_ref, sh_ref, bg_ref, wz_ref, wx_ref, wq_ref, wc_ref, wg_ref, wm_ref,
     wuq_ref, wuk_ref, wuv_ref, gsq_ref, gsk_ref, gqa_ref, gkva_ref, gmq_ref, gmk_ref) = refs[:20]
    rope_refs = refs[20:20 + n_rope]
    (z_ref, xbc_ref, gate_ref, misc_ref, sq_ref, sk_ref, sv_ref, mq_ref, mk_ref, mv_ref) = refs[20 + n_rope:]
    swa_rope = tuple(r[...] for r in rope_refs[0:3]) if with_rope else None
    mla_rope = tuple(r[...] for r in rope_refs[3:6]) if with_rope else None

    h = (_rms(x_ref[...]) * g_ref[...] * (1.0 + sc_ref[0]) + sh_ref[0]).astype(bf16)
    z_ref[...] = jnp.dot(h, wz_ref[...], preferred_element_type=f32)
    xbc_ref[...] = jnp.dot(h, wx_ref[...], preferred_element_type=f32)
    gate_ref[...] = jax.nn.sigmoid(jnp.dot(h, wg_ref[...], preferred_element_type=f32) + bg_ref[...])
    misc = jnp.dot(h, wm_ref[...], preferred_element_type=f32)
    misc_ref[...] = misc

    qkv = jnp.dot(h, wq_ref[...], preferred_element_type=f32)
    slot = lambda a, i: a[:, i * LANES:(i + 1) * LANES]
    for hd in range(SWA_HEADS):
        y = _slot_norm_rope(slot(qkv, hd), SWA_HEAD_DIM, gsq_ref[...], swa_rope, SWA_HEAD_DIM // 2)
        sq_ref[0, hd] = y[:, :SWA_HEAD_DIM].astype(bf16)
    for hd in range(SWA_KV_HEADS):
        y = _slot_norm_rope(slot(qkv, SWA_HEADS + hd), SWA_HEAD_DIM, gsk_ref[...], swa_rope, SWA_HEAD_DIM // 2)
        sk_ref[0, hd] = y[:, :SWA_HEAD_DIM].astype(bf16)
        sv_ref[0, hd] = slot(qkv, SWA_HEADS + SWA_KV_HEADS + hd)[:, :SWA_HEAD_DIM].astype(bf16)

    cq = jnp.dot(h, wc_ref[...], preferred_element_type=f32)
    qa = (_rms(cq[:, :MLA_Q_RANK]) * gqa_ref[...]).astype(bf16)
    kva = (_rms(cq[:, MLA_Q_RANK:]) * gkva_ref[...]).astype(bf16)
    q_up = jnp.dot(qa, wuq_ref[...], preferred_element_type=f32)
    k_up = jnp.dot(kva, wuk_ref[...], preferred_element_type=f32)
    v_up = jnp.dot(kva, wuv_ref[...], preferred_element_type=f32)
    lane = lax.broadcasted_iota(jnp.int32, misc.shape, 1)
    k_rope = jnp.where((lane >= MISC_ROPE_AT) & (lane < MISC_ROPE_AT + MLA_ROPE), misc, 0.0)
    for hd in range(MLA_HEADS):
        mq_ref[0, hd] = _slot_norm_rope(slot(q_up, hd), MLA_QK, gmq_ref[...], mla_rope,
                                        MLA_ROPE // 2).astype(bf16)
        mk_ref[0, hd] = _slot_norm_rope(slot(k_up, hd) + k_rope, MLA_QK, gmk_ref[...], mla_rope,
                                        MLA_ROPE // 2).astype(bf16)
        mv_ref[0, hd] = slot(v_up, hd)[:, :MLA_V].astype(bf16)


def _slot_pack(w, n_heads, width):
    k = w.shape[0]
    w3 = w.reshape(k, n_heads, width)
    return jnp.pad(w3, ((0, 0), (0, 0), (0, LANES - width))).reshape(k, n_heads * LANES)


def _slot_row(v, width_to=LANES):
    v = v.astype(jnp.float32)
    return jnp.pad(v, (0, width_to - v.shape[0])).reshape(1, width_to)


def _pack_weights(p):
    w_in = p['w_in']
    pts = [0] + split_points() + [IN_WIDTH]
    col = lambda i: w_in[:, pts[i]:pts[i + 1]]
    bf16 = jnp.bfloat16
    zeros = jnp.zeros((w_in.shape[0], MISC_ROPE_AT), w_in.dtype)
    qkv = jnp.concatenate([_slot_pack(col(4), SWA_HEADS, SWA_HEAD_DIM),
                           _slot_pack(col(5), SWA_KV_HEADS, SWA_HEAD_DIM),
                           _slot_pack(col(6), SWA_KV_HEADS, SWA_HEAD_DIM)], axis=1)
    ukv = p['mla_w_ukv'].reshape(MLA_KV_RANK, MLA_HEADS, MLA_NOPE + MLA_V)
    weights = (col(0), col(1), qkv, jnp.concatenate([col(7), col(8)], axis=1),
               jnp.concatenate([col(10), col(11), col(12)], axis=1),
               jnp.concatenate([zeros, col(9), col(2), col(3)], axis=1),
               _slot_pack(p['mla_w_uq'], MLA_HEADS, MLA_QK),
               _slot_pack(ukv[:, :, :MLA_NOPE].reshape(MLA_KV_RANK, -1), MLA_HEADS, MLA_NOPE),
               _slot_pack(ukv[:, :, MLA_NOPE:].reshape(MLA_KV_RANK, -1), MLA_HEADS, MLA_V))
    gains = (_slot_row(p['swa_q_norm'] * SWA_SCALE), _slot_row(p['swa_k_norm']),
             _slot_row(p['mla_q_a_norm'], MLA_Q_RANK), _slot_row(p['mla_kv_a_norm'], MLA_KV_RANK),
             _slot_row(p['mla_q_norm'] * MLA_SCALE), _slot_row(p['mla_k_norm']))
    return tuple(w.astype(bf16) for w in weights), gains


def _rope_slot_tables(cos, sin, start):
    n, half = cos.shape
    ones = lambda w: jnp.ones((n, w), jnp.float32)
    zeros = lambda w: jnp.zeros((n, w), jnp.float32)
    tail = LANES - start - 2 * half
    c = jnp.concatenate([ones(start), cos, cos, ones(tail)], axis=1)
    s_lo = jnp.concatenate([zeros(start), -sin, zeros(half), zeros(tail)], axis=1)
    s_hi = jnp.concatenate([zeros(start), zeros(half), sin, zeros(tail)], axis=1)
    return c, s_lo, s_hi


def _batch_row_spec(width, tiles_per_batch):
    return pl.BlockSpec((1, 1, width), lambda t: (t // tiles_per_batch, 0, 0))


def premix(x, norm_g, scale, shift, b_gate, packed, rope_tables):
    bsz, t_len, d = x.shape
    n = bsz * t_len
    tm = min(PROJ_TM, t_len)
    tpb = t_len // tm
    f32 = jnp.float32
    bf16 = jnp.bfloat16
    weights, gains = packed
    const = lambda a: pl.BlockSpec(a.shape, lambda t: (0,) * a.ndim)
    tok = lambda w: pl.BlockSpec((tm, w), lambda t: (t, 0))
    heads = lambda nh, w: pl.BlockSpec((1, nh, tm, w), lambda t: (t // tpb, 0, t % tpb, 0))
    with_rope = rope_tables is not None
    rope_in = list(rope_tables) if with_rope else []
    rope_specs = [pl.BlockSpec((tm, LANES), lambda t: (t % tpb, 0)) for _ in rope_in]
    flat_w = (SSD_D_INNER, SSD_CONV_CH, 3 * D_MODEL, MISC_W)
    head_shapes = ((SWA_HEADS, SWA_HEAD_DIM), (SWA_KV_HEADS, SWA_HEAD_DIM), (SWA_KV_HEADS, SWA_HEAD_DIM),
                   (MLA_HEADS, LANES), (MLA_HEADS, LANES), (MLA_HEADS, MLA_V))
    small = [norm_g.reshape(1, d), scale, shift, b_gate.reshape(1, 3 * D_MODEL)]
    outs = pl.pallas_call(
        functools.partial(_premix_kernel, with_rope=with_rope),
        grid=(n // tm,),
        in_specs=([tok(d), const(small[0]), _batch_row_spec(d, tpb), _batch_row_spec(d, tpb), const(small[3])]
                  + [const(w) for w in weights] + [const(g) for g in gains] + rope_specs),
        out_specs=[tok(w) for w in flat_w] + [heads(nh, w) for nh, w in head_shapes],
        out_shape=([jax.ShapeDtypeStruct((n, w), f32) for w in flat_w]
                   + [jax.ShapeDtypeStruct((bsz, nh, t_len, w), bf16) for nh, w in head_shapes]),
        compiler_params=pltpu.CompilerParams(dimension_semantics=("parallel",),
                                             vmem_limit_bytes=PEER_VMEM_LIMIT),
        name="premix",
    )(x.reshape(n, d), *small, *weights, *gains, *rope_in)
    return [o.reshape(bsz, t_len, -1) for o in outs[:4]] + list(outs[4:])


def _heads_to_lanes(y_ref):
    return jnp.concatenate([y_ref[0, h].astype(jnp.bfloat16) for h in range(y_ref.shape[1])], axis=-1)


def _merge_kernel(x_ref, yf_ref, yb_ref, z_ref, yw_ref, ym_ref, gate_ref, gs_ref, g1_ref, n2_ref, sc_ref,
                  sh_ref, w1_ref, w2_ref, w3_ref, wo_ref, xo_ref, h2_ref):
    f32 = jnp.float32
    bf16 = jnp.bfloat16
    d = D_MODEL
    z = z_ref[...]
    ys = _rms((yf_ref[...] + yb_ref[...]) * (z * jax.nn.sigmoid(z))) * gs_ref[...]
    m = (gate_ref[:, 0:d] * jnp.dot(ys.astype(bf16), w1_ref[...], preferred_element_type=f32)
         + gate_ref[:, d:2 * d] * jnp.dot(_heads_to_lanes(yw_ref), w2_ref[...], preferred_element_type=f32)
         + gate_ref[:, 2 * d:3 * d] * jnp.dot(_heads_to_lanes(ym_ref), w3_ref[...], preferred_element_type=f32))
    mix = jnp.dot(m.astype(bf16), wo_ref[...], preferred_element_type=f32)
    xn = x_ref[...] + g1_ref[0] * mix
    xo_ref[...] = xn
    h2_ref[...] = (_rms(xn) * n2_ref[...] * (1.0 + sc_ref[0]) + sh_ref[0]).astype(bf16)


def merge_norm(x, yf, yb, y_block, z, yw, ym, gates, p, g1, sc2, sh2):
    bsz, t_len, d = x.shape
    n = bsz * t_len
    tm = min(PROJ_TM, t_len)
    bf16 = jnp.bfloat16
    const = lambda shape: pl.BlockSpec(shape, lambda t: (0,) * len(shape))
    tok = lambda w: pl.BlockSpec((tm, w), lambda t: (t, 0))
    brow = _batch_row_spec(d, t_len // tm)
    ytok = pl.BlockSpec((tm, d), lambda t: (y_block(t, tm), 0))
    tpb = t_len // tm
    heads = lambda a: pl.BlockSpec((1, a.shape[1], tm, a.shape[3]), lambda t: (t // tpb, 0, t % tpb, 0))
    flat = lambda a: a.reshape(-1, a.shape[-1])
    swa_w = SWA_HEADS * SWA_HEAD_DIM
    mla_w = MLA_HEADS * MLA_V
    xo, h2 = pl.pallas_call(
        _merge_kernel,
        grid=(n // tm,),
        in_specs=[tok(d), ytok, ytok, tok(d), heads(yw), heads(ym), tok(3 * d),
                  const((1, d)), brow, const((1, d)), brow, brow,
                  const((d, d)), const((swa_w, d)), const((mla_w, d)), const((d, d))],
        out_specs=[tok(d), tok(d)],
        out_shape=[jax.ShapeDtypeStruct((n, d), jnp.float32), jax.ShapeDtypeStruct((n, d), bf16)],
        compiler_params=pltpu.CompilerParams(dimension_semantics=("parallel",),
                                             vmem_limit_bytes=PEER_VMEM_LIMIT),
        name="merge_norm",
    )(flat(x), flat(yf), flat(yb), flat(z), yw, ym, flat(gates),
      p['ssd_norm_g'].reshape(1, d), g1, p['norm2_g'].reshape(1, d), sc2, sh2,
      p['w_br_ssd'].astype(bf16), p['w_br_swa'].astype(bf16), p['w_br_mla'].astype(bf16),
      p['w_out'].astype(bf16))
    return xo.reshape(bsz, t_len, d), h2.reshape(bsz, t_len, d)


def split_points():
    pts, acc = [], 0
    for s in IN_SIZES[:-1]:
        acc += s
        pts.append(acc)
    return pts


def rms_norm(x, g):
    xf = x.astype(jnp.float32)
    y = xf * lax.rsqrt(jnp.mean(xf * xf, axis=-1, keepdims=True) + EPS)
    return (y * g.astype(jnp.float32)).astype(x.dtype)


def axial_rope(rows, dim):
    pairs = dim // 4
    freqs = ROPE_BASE ** (-jnp.arange(pairs, dtype=jnp.float32) / pairs)
    pos_r = jnp.repeat(jnp.arange(rows, dtype=jnp.float32), GRID_W)
    pos_c = jnp.tile(jnp.arange(GRID_W, dtype=jnp.float32), rows)
    ang = jnp.concatenate([pos_r[:, None] * freqs, pos_c[:, None] * freqs], axis=-1)
    return jnp.cos(ang), jnp.sin(ang)


def apply_rope(x, cos, sin):
    d2 = x.shape[-1] // 2
    x1, x2 = x[..., :d2], x[..., d2:]
    cs, sn = cos[:, None, :], sin[:, None, :]
    return jnp.concatenate([x1 * cs - x2 * sn, x2 * cs + x1 * sn], axis=-1).astype(x.dtype)


def dwconv_centred(u, w, b):
    pad = (SSD_CONV - 1) // 2
    y = lax.conv_general_dilated(u, w[:, None, :].astype(u.dtype), window_strides=(1,),
                                 padding=[(pad, pad)], dimension_numbers=('NWC', 'WIO', 'NWC'),
                                 feature_group_count=u.shape[-1])
    return y + b


def ssd_mixer(lat, ctx, p, with_ctx_out):
    f32 = jnp.float32

    def conv_act(xbc):
        return jax.nn.silu(dwconv_centred(xbc, p['ssd_conv_w'], p['ssd_conv_b']))

    n_ctx = ctx[1].shape[1]
    u_all = jnp.concatenate([conv_act(ctx[1]), conv_act(lat[1])], axis=1)
    ys = []
    for direction in range(2):
        dt_raw = jnp.concatenate([ctx[2 + direction], lat[2 + direction]], axis=1)
        ys.append(ssd_scan_dir(u_all, dt_raw, p['ssd_dt_bias'][direction], p['ssd_a_log'][direction],
                               p['ssd_d'][direction], direction == 1, n_ctx))
    return ys


def swa_mixer(lat, ctx, p, with_ctx_out):
    ql, kl, vl = lat
    qc, kc, vc = ctx
    y_l = swa_attention(ql, kl, vl, kc, vc, p['swa_sink'])
    y_c = None
    if with_ctx_out:
        n_ctx = qc.shape[2]
        y_c = flash_attention(qc, kc, vc, n_ctx, n_ctx, sink=p['swa_sink'])
    return y_l, y_c


def mla_mixer(lat, ctx, with_ctx_out):
    ql, kl, vl = lat
    qc, kc, vc = ctx
    k_all = jnp.concatenate([kc, kl], axis=2)
    v_all = jnp.concatenate([vc, vl], axis=2)
    y_l = flash_attention(ql, k_all, v_all, MLA_TQ, MLA_TK)
    y_c = None
    if with_ctx_out:
        n_ctx = qc.shape[2]
        y_c = flash_attention(qc, kc, vc, n_ctx, n_ctx)
    return y_l, y_c


PEER_ROUTE_TM = 256
PEER_TM = 512
PEER_EC = 1024
PEER_VMEM_LIMIT = 56 * 1024 * 1024
_SQRT_HALF = 0.7071067811865476


def _top16_rows(s):
    rows = []
    for _ in range(PEER_TOPK):
        m = jnp.max(s, axis=0, keepdims=True)
        rows.append(m)
        s = jnp.where(s == m, -jnp.inf, s)
    return jnp.concatenate(rows, axis=0)


def _peer_route_kernel(h_ref, wq_ref, keys_ref, s0_ref, s1_ref, a_ref, b_ref, tau_ref, q_scr):
    f32 = jnp.float32
    q = jnp.dot(h_ref[...].astype(jnp.bfloat16), wq_ref[...], preferred_element_type=f32)
    q_scr[...] = q.astype(jnp.bfloat16)

    def head_body(hd, carry):
        def scores(p):
            col = pl.multiple_of((hd * 2 + p) * PEER_HALF, PEER_HALF)
            qs = q_scr[:, pl.ds(col, PEER_HALF)]
            return lax.dot_general(keys_ref[p * PEER_HEADS + hd], qs, (((1,), (1,)), ((), ())),
                                   preferred_element_type=f32)

        s0 = scores(0)
        s1 = scores(1)
        a_top = _top16_rows(s0)
        b_top = _top16_rows(s1)
        cand = [a_top[0:1] + b_top]
        cand += [a_top[r:r + 1] + b_top[0:8] for r in range(1, 8)]
        cand.append(a_top[8:16] + b_top[0:1])
        best = _top16_rows(jnp.concatenate(cand, axis=0))
        z = jnp.sum(jnp.exp(best - best[0:1]), axis=0, keepdims=True)
        a = jnp.exp(s0 - a_top[0:1])
        b = jnp.exp(s1 - b_top[0:1]) / z
        tau = best[PEER_TOPK - 1:PEER_TOPK]
        for tl in range(s0.shape[1] // LANES):
            ts = slice(tl * LANES, (tl + 1) * LANES)
            s0_ref[hd, tl] = s0[:, ts]
            s1_ref[hd, tl] = s1[:, ts]
            a_ref[hd, tl] = a[:, ts]
            b_ref[hd, tl] = b[:, ts]
            tau_ref[hd, tl] = tau[:, ts]
        return carry

    lax.fori_loop(0, PEER_HEADS, head_body, 0, unroll=4)


def _gelu_exact(x):
    return 0.5 * x * (1.0 + lax.erf(x * _SQRT_HALF))


def _peer_expert_kernel(h_ref, u_ref, vt_ref, s0_ref, s1_ref, a_ref, b_ref, tau_ref, x_ref, g2_ref,
                        o_ref, acc_ref, act_scr, g_scr):
    f32 = jnp.float32
    j = pl.program_id(1)
    rows_per_step = PEER_EC // PEER_N_KEYS

    @pl.when(j == 0)
    def _():
        acc_ref[...] = jnp.zeros_like(acc_ref)

    act_scr[...] = lax.dot_general(u_ref[...], h_ref[...], (((1,), (1,)), ((), ())),
                                   preferred_element_type=f32)
    n_tl = PEER_TM // LANES

    def tile_body(t, carry):
        ii = t // n_tl
        tl = t % n_tl
        i = j * rows_per_step + ii
        w = jnp.zeros((PEER_N_KEYS, LANES), f32)
        for hd in range(PEER_HEADS):
            s0_row = s0_ref[hd, tl, pl.ds(i, 1), :]
            a_row = a_ref[hd, tl, pl.ds(i, 1), :]
            keep = (s1_ref[hd, tl] + s0_row) >= tau_ref[hd, tl]
            w = w + jnp.where(keep, b_ref[hd, tl], 0.0) * a_row
        es = pl.ds(pl.multiple_of(ii * PEER_N_KEYS, PEER_N_KEYS), PEER_N_KEYS)
        ts = pl.ds(pl.multiple_of(tl * LANES, LANES), LANES)
        g_scr[es, ts] = (w * _gelu_exact(act_scr[es, ts])).astype(jnp.bfloat16)
        return carry

    lax.fori_loop(0, rows_per_step * n_tl, tile_body, 0, unroll=4)
    acc_ref[...] += jnp.dot(vt_ref[...], g_scr[...], preferred_element_type=f32)

    @pl.when(j == pl.num_programs(1) - 1)
    def _():
        o_ref[...] = x_ref[...] + g2_ref[0] * acc_ref[...].T


def peer_ffn(h, x, g2_tiles, w_q, sub_keys, u_tab, v_tab):
    bsz, T, D = h.shape
    n_tok = bsz * T
    n_exp = u_tab.shape[0]
    hf = h.reshape(n_tok, D)
    f32 = jnp.float32
    bf16 = jnp.bfloat16
    keys = sub_keys.reshape(2 * PEER_HEADS, PEER_N_KEYS, PEER_HALF).astype(bf16)
    tm = PEER_ROUTE_TM
    n_tiles = n_tok // LANES
    big = jax.ShapeDtypeStruct((PEER_HEADS, n_tiles, PEER_N_KEYS, LANES), f32)
    small = jax.ShapeDtypeStruct((PEER_HEADS, n_tiles, 1, LANES), f32)
    big_spec = pl.BlockSpec((PEER_HEADS, tm // LANES, PEER_N_KEYS, LANES), lambda t: (0, t, 0, 0))
    small_spec = pl.BlockSpec((PEER_HEADS, tm // LANES, 1, LANES), lambda t: (0, t, 0, 0))
    s0, s1, a, b, tau = pl.pallas_call(
        _peer_route_kernel,
        grid=(n_tok // tm,),
        in_specs=[pl.BlockSpec((tm, D), lambda t: (t, 0)),
                  pl.BlockSpec((D, PEER_HEADS * PEER_D_KEY), lambda t: (0, 0)),
                  pl.BlockSpec((2 * PEER_HEADS, PEER_N_KEYS, PEER_HALF), lambda t: (0, 0, 0))],
        out_specs=[big_spec, big_spec, big_spec, big_spec, small_spec],
        out_shape=[big, big, big, big, small],
        scratch_shapes=[pltpu.VMEM((tm, PEER_HEADS * PEER_D_KEY), bf16)],
        compiler_params=pltpu.CompilerParams(dimension_semantics=("parallel",),
                                             vmem_limit_bytes=PEER_VMEM_LIMIT),
        name="peer_route",
    )(hf, w_q.astype(bf16), keys)

    tm = PEER_TM
    ec = PEER_EC
    big_spec = pl.BlockSpec((PEER_HEADS, tm // LANES, PEER_N_KEYS, LANES), lambda t, j: (0, t, 0, 0))
    small_spec = pl.BlockSpec((PEER_HEADS, tm // LANES, 1, LANES), lambda t, j: (0, t, 0, 0))
    out = pl.pallas_call(
        _peer_expert_kernel,
        grid=(n_tok // tm, n_exp // ec),
        in_specs=[pl.BlockSpec((tm, D), lambda t, j: (t, 0)),
                  pl.BlockSpec((ec, D), lambda t, j: (j, 0)),
                  pl.BlockSpec((D, ec), lambda t, j: (0, j)),
                  big_spec, big_spec, big_spec, big_spec, small_spec,
                  pl.BlockSpec((tm, D), lambda t, j: (t, 0)),
                  pl.BlockSpec((1, 1, D), lambda t, j: (t, 0, 0))],
        out_specs=pl.BlockSpec((tm, D), lambda t, j: (t, 0)),
        out_shape=jax.ShapeDtypeStruct((n_tok, D), f32),
        scratch_shapes=[pltpu.VMEM((D, tm), f32), pltpu.VMEM((ec, tm), f32), pltpu.VMEM((ec, tm), bf16)],
        compiler_params=pltpu.CompilerParams(dimension_semantics=("parallel", "arbitrary"),
                                             vmem_limit_bytes=PEER_VMEM_LIMIT),
        name="peer_expert",
    )(hf, u_tab.astype(bf16), v_tab.T.astype(bf16), s0, s1, a, b, tau, x.reshape(n_tok, D), g2_tiles)
    return out.reshape(bsz, T, D)


def _split_cols(t, sizes):
    out, acc = [], 0
    for s in sizes:
        out.append(t[..., acc:acc + s])
        acc += s
    return out


def layer(xl, xc, c, c_ctx, p, rope_tables, with_ctx_out):
    bsz, seq, d = xl.shape
    n_ctx = xc.shape[1]
    mod_l = (jax.nn.silu(c) @ p['w_mod'] + p['b_mod'])[:, None, :]
    mod_c = jnp.broadcast_to((jax.nn.silu(c_ctx) @ p['w_mod'] + p['b_mod'])[None, None, :], mod_l.shape)
    sh1_l, sc1_l, g1_l, sh2_l, sc2_l, g2_l = jnp.split(mod_l, 6, axis=-1)
    sh1_c, sc1_c, g1_c, sh2_c, sc2_c, g2_c = jnp.split(mod_c, 6, axis=-1)

    packed = _pack_weights(p)
    dt_at = MISC_ROPE_AT + MLA_ROPE

    def project(x, sc, sh, rope):
        z, xbc, gates, misc, sq, sk, sv, mq, mk, mv = premix(x, p['norm1_g'], sc, sh, p['b_gate'], packed, rope)
        dtf = misc[..., dt_at:dt_at + SSD_HEADS]
        dtb = misc[..., dt_at + SSD_HEADS:dt_at + 2 * SSD_HEADS]
        return dict(z=z, gates=gates, ssd=(None, xbc, dtf, dtb), swa=(sq, sk, sv), mla=(mq, mk, mv))

    pl_ = project(xl, sc1_l, sh1_l, rope_tables)
    pc = project(xc, sc1_c, sh1_c, None)
    yf, yb = ssd_mixer(pl_['ssd'], pc['ssd'], p, with_ctx_out)
    y_swa_l, y_swa_c = swa_mixer(pl_['swa'], pc['swa'], p, with_ctx_out)
    y_mla_l, y_mla_c = mla_mixer(pl_['mla'], pc['mla'], with_ctx_out)

    def lat_block(t, tm):
        return (t // (seq // tm)) * ((n_ctx + seq) // tm) + n_ctx // tm + t % (seq // tm)

    def ctx_block(t, tm):
        return (t // (n_ctx // tm)) * ((n_ctx + seq) // tm) + t % (n_ctx // tm)

    peer = (p['peer_w_q'], p['peer_keys'], p['peer_u'], p['peer_v'])
    xl, h2_l = merge_norm(xl, yf, yb, lat_block, pl_['z'], y_swa_l, y_mla_l, pl_['gates'], p,
                          g1_l, sc2_l, sh2_l)
    if with_ctx_out:
        xc, h2_c = merge_norm(xc, yf, yb, ctx_block, pc['z'], y_swa_c, y_mla_c, pc['gates'], p,
                              g1_c, sc2_c, sh2_c)
        g2_c_tiles = jnp.broadcast_to(g2_c[:1], (bsz * n_ctx // PEER_TM, 1, d))
        xc = peer_ffn(h2_c, xc, g2_c_tiles, *peer)
    xl = peer_ffn(h2_l, xl, jnp.repeat(g2_l, seq // PEER_TM, axis=0), *peer)
    return xl, xc


def kernel(x, c, ctx, c_ctx, w_mod, b_mod, norm1_g, norm2_g, w_in, ssd_conv_w, ssd_conv_b, ssd_dt_bias,
           ssd_a_log, ssd_d, ssd_norm_g, swa_q_norm, swa_k_norm, swa_sink, mla_q_a_norm, mla_kv_a_norm,
           mla_w_uq, mla_w_ukv, mla_q_norm, mla_k_norm, b_gate, w_br_ssd, w_br_swa, w_br_mla, w_out,
           peer_w_q, peer_keys, peer_u, peer_v):
    L = x.shape[1]
    rows = L // GRID_W
    rope_tables = (_rope_slot_tables(*axial_rope(rows, SWA_HEAD_DIM), 0)
                   + _rope_slot_tables(*axial_rope(rows, MLA_ROPE), MLA_NOPE))
    params = dict(w_mod=w_mod, b_mod=b_mod, norm1_g=norm1_g, norm2_g=norm2_g, w_in=w_in,
                  ssd_conv_w=ssd_conv_w, ssd_conv_b=ssd_conv_b, ssd_dt_bias=ssd_dt_bias,
                  ssd_a_log=ssd_a_log, ssd_d=ssd_d, ssd_norm_g=ssd_norm_g, swa_q_norm=swa_q_norm,
                  swa_k_norm=swa_k_norm, swa_sink=swa_sink, mla_q_a_norm=mla_q_a_norm,
                  mla_kv_a_norm=mla_kv_a_norm, mla_w_uq=mla_w_uq, mla_w_ukv=mla_w_ukv,
                  mla_q_norm=mla_q_norm, mla_k_norm=mla_k_norm, b_gate=b_gate, w_br_ssd=w_br_ssd,
                  w_br_swa=w_br_swa, w_br_mla=w_br_mla, w_out=w_out, peer_w_q=peer_w_q,
                  peer_keys=peer_keys, peer_u=peer_u, peer_v=peer_v)
    xl, xc = x, ctx
    for i in range(DEPTH):
        p = {k: v[i] for k, v in params.items()}
        xl, xc = layer(xl, xc, c, c_ctx, p, rope_tables, i < DEPTH - 1)
    return xl
```
